```python
import math
import jax
import jax.numpy as jnp
from jax import lax
import numpy as np

D_MODEL = 2048
BATCH = 4
SEQ = 2048
DEPTH = 2

A_HEADS = 4
A_DH = 64
B_HEADS = 4
B_DH = 128
C_HEADS = 4
C_DH = 128
D_HEADS = 4
D_DH = 128
IDX_HEADS = 8
IDX_DH = 64
N_BRANCH = 4
BRANCH_W = D_MODEL // 4
Q_BLOCK = 128
MOBA_BLOCK = 256
MOBA_TOPK = 3
MOBA_QCHUNK = 32
DSA_TOPK = 256
N_EXPERTS = 32
TOP_K = 4
D_FF = D_MODEL
SWIGLU_ALPHA = 1.702
SWIGLU_LIMIT = 7.0
MOE_BLOCK = 256
N_ALIBI = A_HEADS + C_HEADS + D_HEADS
RMS_EPS = 1e-6
IN_SIZES = (
    A_HEADS * 2 * A_DH, A_HEADS * 2 * A_DH, A_HEADS * 2 * A_DH,
    B_HEADS * B_DH, B_HEADS * B_DH, B_HEADS * B_DH,
    C_HEADS * C_DH, C_HEADS * C_DH, C_HEADS * C_DH,
    D_HEADS * D_DH, D_DH, D_DH, IDX_HEADS * IDX_DH, IDX_DH, IDX_HEADS,
    N_BRANCH * D_MODEL,
)
N_IN = sum(IN_SIZES)

kernel_name = 'hybrid_gated_mixers_adaln_moe'


def rms_norm(x, g):
    xf = x.astype(jnp.float32)
    y = xf * lax.rsqrt(jnp.mean(xf * xf, axis=-1, keepdims=True) + RMS_EPS)
    return (y * g.astype(jnp.float32)).astype(x.dtype)


def alibi_slopes():
    return jnp.exp2(-8.0 * (jnp.arange(N_ALIBI, dtype=jnp.float32) + 1.0) / N_ALIBI)


def diff_attention(q, k, v, slopes, lam, lambda_init, subln_g):
    bsz, seq = q.shape[:2]
    scale = A_DH ** -0.5
    qh = q.transpose(0, 2, 3, 1, 4)
    kh = k.transpose(0, 2, 3, 1, 4)
    vh = v.transpose(0, 2, 1, 3)
    outs = []
    for i in range(seq // Q_BLOCK):
        lo, hi = i * Q_BLOCK, (i + 1) * Q_BLOCK
        dist = (jnp.arange(lo, hi)[:, None] - jnp.arange(hi)[None, :]).astype(jnp.float32)
        bias = -slopes[:, None, None] * dist
        s = jnp.einsum('bhcqd,bhckd->bhcqk', qh[:, :, :, lo:hi], kh[:, :, :, :hi]).astype(jnp.float32) * scale
        s = jnp.where(dist >= 0, s + bias[None, :, None], -jnp.inf)
        p = jax.nn.softmax(s, axis=-1)
        w = p[:, :, 0] - lam * p[:, :, 1]
        outs.append(jnp.einsum('bhqk,bhkd->bhqd', w.astype(vh.dtype), vh[:, :, :hi]))
    o = jnp.concatenate(outs, axis=2)
    o = rms_norm(o, subln_g) * (1.0 - lambda_init)
    return o.transpose(0, 2, 1, 3).reshape(bsz, seq, -1)


def stick_breaking_attention(q, k, v):
    bsz, seq, _, dh = q.shape
    scale = dh ** -0.5
    qh, kh, vh = (t.transpose(0, 2, 1, 3) for t in (q, k, v))
    outs = []
    for i in range(seq // Q_BLOCK):
        lo, hi = i * Q_BLOCK, (i + 1) * Q_BLOCK
        mask = jnp.arange(hi)[None, :] < jnp.arange(lo, hi)[:, None]
        z = jnp.einsum('bhqd,bhkd->bhqk', qh[:, :, lo:hi], kh[:, :, :hi]).astype(jnp.float32) * scale
        log_1mb = jnp.where(mask, jax.nn.log_sigmoid(-z), 0.0)
        after = lax.cumsum(log_1mb, axis=3, reverse=True) - log_1mb
        w = jnp.where(mask, jnp.exp(jax.nn.log_sigmoid(z) + after), 0.0)
        outs.append(jnp.einsum('bhqk,bhkd->bhqd', w.astype(vh.dtype), vh[:, :, :hi]))
    o = jnp.concatenate(outs, axis=2)
    return o.transpose(0, 2, 1, 3).reshape(bsz, seq, -1)


def moba_attention(q, k, v, slopes):
    bsz, seq, nh, dh = q.shape
    scale = dh ** -0.5
    n_kb = -(-seq // MOBA_BLOCK)
    pad = n_kb * MOBA_BLOCK - seq
    n_sel = min(MOBA_TOPK, n_kb - 1)
    qh, kh, vh = (t.transpose(0, 2, 1, 3) for t in (q, k, v))
    kp = jnp.pad(kh, ((0, 0), (0, 0), (0, pad), (0, 0)))
    vp = jnp.pad(vh, ((0, 0), (0, 0), (0, pad), (0, 0)))
    kb = kp.reshape(bsz, nh, n_kb, MOBA_BLOCK, dh)
    vb = vp.reshape(bsz, nh, n_kb, MOBA_BLOCK, dh)
    k_mean = jnp.mean(kb.astype(jnp.float32), axis=3).astype(q.dtype)
    bi = jnp.arange(bsz)[:, None, None, None]
    hi = jnp.arange(nh)[None, :, None, None]
    sl4 = slopes[None, :, None, None]

    def chunk(j):
        t0 = j * MOBA_QCHUNK
        blk = t0 // MOBA_BLOCK
        tq = t0 + jnp.arange(MOBA_QCHUNK)
        qc = lax.dynamic_slice_in_dim(qh, t0, MOBA_QCHUNK, axis=2)
        b0 = blk * MOBA_BLOCK
        k_own = lax.dynamic_slice_in_dim(kp, b0, MOBA_BLOCK, axis=2)
        v_own = lax.dynamic_slice_in_dim(vp, b0, MOBA_BLOCK, axis=2)
        dist_own = (tq[:, None] - (b0 + jnp.arange(MOBA_BLOCK))[None, :]).astype(jnp.float32)
        s_own = jnp.einsum('bhqd,bhkd->bhqk', qc, k_own).astype(jnp.float32) * scale - sl4 * dist_own
        s_own = jnp.where(dist_own >= 0, s_own, -jnp.inf)
        if n_sel == 0:
            p = jax.nn.softmax(s_own, axis=-1)
            return jnp.einsum('bhqk,bhkd->bhqd', p.astype(v.dtype), v_own)
        gate = jnp.einsum('bhqd,bhnd->bhqn', qc, k_mean).astype(jnp.float32)
        gate = jnp.where(jnp.arange(n_kb) < blk, gate, -jnp.inf)
        _, sel = lax.top_k(gate, n_sel)
        sel_ok = jnp.arange(n_sel) < blk
        k_sel = kb[bi, hi, sel]
        v_sel = vb[bi, hi, sel]
        pos_sel = sel[..., None] * MOBA_BLOCK + jnp.arange(MOBA_BLOCK)
        dist_sel = (tq[None, None, :, None, None] - pos_sel).astype(jnp.float32)
        s_sel = jnp.einsum('bhqd,bhqnkd->bhqnk', qc, k_sel).astype(jnp.float32) * scale - slopes[None, :, None, None, None] * dist_sel
        s_sel = jnp.where(sel_ok[:, None], s_sel, -jnp.inf).reshape(bsz, nh, MOBA_QCHUNK, n_sel * MOBA_BLOCK)
        p = jax.nn.softmax(jnp.concatenate([s_sel, s_own], axis=-1), axis=-1)
        p_sel = p[..., :n_sel * MOBA_BLOCK].reshape(bsz, nh, MOBA_QCHUNK, n_sel, MOBA_BLOCK)
        p_own = p[..., n_sel * MOBA_BLOCK:]
        return (jnp.einsum('bhqnk,bhqnkd->bhqd', p_sel.astype(v.dtype), v_sel)
                + jnp.einsum('bhqk,bhkd->bhqd', p_own.astype(v.dtype), v_own))

    outs = lax.map(chunk, jnp.arange(seq // MOBA_QCHUNK))
    o = outs.transpose(1, 2, 0, 3, 4).reshape(bsz, nh, seq, dh)
    return o.transpose(0, 2, 1, 3).reshape(bsz, seq, nh * dh)


def dsa_attention(q, k, v, q_idx, k_idx, w_idx, slopes):
    bsz, seq, nh, dh = q.shape
    scale = dh ** -0.5
    n_keep = min(DSA_TOPK, seq // 4)
    bi = jnp.arange(bsz)[:, None, None]

    def block(i):
        t0 = i * Q_BLOCK
        tq = t0 + jnp.arange(Q_BLOCK)
        qc = lax.dynamic_slice_in_dim(q, t0, Q_BLOCK, axis=1)
        qic = lax.dynamic_slice_in_dim(q_idx, t0, Q_BLOCK, axis=1)
        wc = lax.dynamic_slice_in_dim(w_idx, t0, Q_BLOCK, axis=1)
        logits = jnp.einsum('bqhd,bsd->bqhs', qic, k_idx).astype(jnp.float32) * IDX_DH ** -0.5
        score = jnp.einsum('bqhs,bqh->bqs', jax.nn.relu(logits), wc.astype(jnp.float32) * IDX_HEADS ** -0.5)
        score = jnp.where(jnp.arange(seq)[None, :] <= tq[:, None], score, -jnp.inf)
        _, idx = lax.top_k(score, n_keep)
        k_g = k[bi, idx]
        v_g = v[bi, idx]
        dist = (tq[None, :, None] - idx).astype(jnp.float32)[:, None]
        s = jnp.einsum('bqhd,bqkd->bhqk', qc, k_g).astype(jnp.float32) * scale - slopes[None, :, None, None] * dist
        s = jnp.where(dist >= 0, s, -jnp.inf)
        p = jax.nn.softmax(s, axis=-1)
        return jnp.einsum('bhqk,bqkd->bqhd', p.astype(v.dtype), v_g)

    outs = lax.map(block, jnp.arange(seq // Q_BLOCK))
    return outs.transpose(1, 0, 2, 3, 4).reshape(bsz, seq, nh * dh)


def token_mixer(h, l, w_in, gate_b, a_qn_g, a_kn_g, a_lam_q1, a_lam_k1, a_lam_q2, a_lam_k2,
                a_subln_g, c_qn_g, c_kn_g, d_qn_g, d_kn_g, w_branch, w_out, slopes):
    bsz, seq, _ = h.shape
    f32 = jnp.float32
    proj = h @ w_in[l]
    splits = [int(s) for s in np.cumsum(IN_SIZES)[:-1]]
    (aq, ak, av, bq, bk, bv, cq, ck, cv, dq, dk, dv, iq, ik, iw, gl) = jnp.split(proj, splits, axis=-1)
    s_a, s_c, s_d = slopes[0::3], slopes[1::3], slopes[2::3]
    lambda_init = 0.8 - 0.6 * math.exp(-0.3 * l)
    lam = (jnp.exp(jnp.sum(a_lam_q1[l].astype(f32) * a_lam_k1[l].astype(f32)))
           - jnp.exp(jnp.sum(a_lam_q2[l].astype(f32) * a_lam_k2[l].astype(f32))) + lambda_init)
    aq = rms_norm(aq.reshape(bsz, seq, A_HEADS, 2, A_DH), a_qn_g[l])
    ak = rms_norm(ak.reshape(bsz, seq, A_HEADS, 2, A_DH), a_kn_g[l])
    oa = diff_attention(aq, ak, av.reshape(bsz, seq, A_HEADS, 2 * A_DH), s_a, lam, lambda_init, a_subln_g[l])
    ob = stick_breaking_attention(bq.reshape(bsz, seq, B_HEADS, B_DH), bk.reshape(bsz, seq, B_HEADS, B_DH),
                                  bv.reshape(bsz, seq, B_HEADS, B_DH))
    cq = rms_norm(cq.reshape(bsz, seq, C_HEADS, C_DH), c_qn_g[l])
    ck = rms_norm(ck.reshape(bsz, seq, C_HEADS, C_DH), c_kn_g[l])
    oc = moba_attention(cq, ck, cv.reshape(bsz, seq, C_HEADS, C_DH), s_c)
    dq = rms_norm(dq.reshape(bsz, seq, D_HEADS, D_DH), d_qn_g[l])
    dk = rms_norm(dk, d_kn_g[l])
    od = dsa_attention(dq, dk, dv, iq.reshape(bsz, seq, IDX_HEADS, IDX_DH), ik, iw, s_d)
    branches = jnp.stack([oa, ob, oc, od], axis=2)
    up = jnp.einsum('bsnw,nwd->bsnd', branches, w_branch[l])
    gates = jax.nn.sigmoid(gl + gate_b[l]).reshape(bsz, seq, N_BRANCH, D_MODEL)
    y = jnp.sum(gates * up, axis=2)
    return y @ w_out[l]


def moe_ffn(h, l, router_w, router_b, w1, b1, w2, b2):
    bsz, seq, dm = h.shape
    xt = h.reshape(-1, dm)
    n_tok = xt.shape[0]
    logits = (xt @ router_w[l]).astype(jnp.float32) + router_b[l].astype(jnp.float32)
    top_val, top_idx = lax.top_k(logits, TOP_K)
    gate = jax.nn.softmax(top_val, axis=-1).astype(xt.dtype)
    n_assign = n_tok * TOP_K
    e_flat = top_idx.reshape(-1)
    tok_flat = jnp.arange(n_assign) // TOP_K
    g_flat = gate.reshape(-1)
    order = jnp.argsort(e_flat)
    e_sorted = e_flat[order]
    counts = jnp.zeros((N_EXPERTS,), jnp.int32).at[e_flat].add(1)
    padded = (counts + MOE_BLOCK - 1) // MOE_BLOCK * MOE_BLOCK
    start = jnp.cumsum(counts) - counts
    pend = jnp.cumsum(padded)
    pstart = pend - padded
    dest = pstart[e_sorted] + (jnp.arange(n_assign) - start[e_sorted])
    n_blocks = -(-(n_assign + N_EXPERTS * (MOE_BLOCK - 1)) // MOE_BLOCK)
    n_rows = n_blocks * MOE_BLOCK
    row_tok = jnp.zeros((n_rows,), jnp.int32).at[dest].set(tok_flat[order])
    row_gate = jnp.zeros((n_rows,), xt.dtype).at[dest].set(g_flat[order])
    block_start = jnp.arange(n_blocks) * MOE_BLOCK
    block_exp = jnp.minimum(jnp.sum(pend[None, :] <= block_start[:, None], axis=1), N_EXPERTS - 1)

    def expert_block(args):
        e, toks, g = args
        hb = xt[toks] @ w1[l, e] + b1[l, e]
        x_glu = jnp.minimum(hb[:, :D_FF], SWIGLU_LIMIT)
        x_lin = jnp.clip(hb[:, D_FF:], -SWIGLU_LIMIT, SWIGLU_LIMIT)
        act = x_glu * jax.nn.sigmoid(SWIGLU_ALPHA * x_glu) * (x_lin + 1.0)
        return (act @ w2[l, e] + b2[l, e]) * g[:, None]

    y_rows = lax.map(expert_block, (block_exp, row_tok.reshape(n_blocks, MOE_BLOCK),
                                    row_gate.reshape(n_blocks, MOE_BLOCK)))
    out = jnp.zeros_like(xt).at[row_tok].add(y_rows.reshape(n_rows, dm))
    return out.reshape(bsz, seq, dm)


def setup_inputs(seed: int = 0) -> dict:
    key = jax.random.key(seed)
    ks = jax.random.split(key, 32)
    L, D = DEPTH, D_MODEL

    def nrm(k, shape, scale):
        return jax.random.normal(k, shape, jnp.float32) * scale

    def gain(k, shape):
        return 1.0 + 0.1 * jax.random.normal(k, shape, jnp.float32)

    return {
        'x': nrm(ks[0], (BATCH, SEQ, D), 1.0),
        'c': nrm(ks[1], (BATCH, D), 1.0),
        'ada_w': nrm(ks[2], (L, D, 6 * D), 0.5 * D ** -0.5),
        'ada_b': nrm(ks[3], (L, 6 * D), 0.02),
        'norm1_g': gain(ks[4], (L, D)),
        'norm2_g': gain(ks[5], (L, D)),
        'w_in': nrm(ks[6], (L, D, N_IN), D ** -0.5),
        'gate_b': nrm(ks[7], (L, N_BRANCH * D), 0.1),
        'a_qn_g': gain(ks[8], (L, A_DH)),
        'a_kn_g': gain(ks[9], (L, A_DH)),
        'a_lam_q1': nrm(ks[10], (L, A_DH), 0.1),
        'a_lam_k1': nrm(ks[11], (L, A_DH), 0.1),
        'a_lam_q2': nrm(ks[12], (L, A_DH), 0.1),
        'a_lam_k2': nrm(ks[13], (L, A_DH), 0.1),
        'a_subln_g': gain(ks[14], (L, 2 * A_DH)),
        'c_qn_g': gain(ks[15], (L, C_DH)),
        'c_kn_g': gain(ks[16], (L, C_DH)),
        'd_qn_g': gain(ks[17], (L, D_DH)),
        'd_kn_g': gain(ks[18], (L, D_DH)),
        'w_branch': nrm(ks[19], (L, N_BRANCH, BRANCH_W, D), BRANCH_W ** -0.5),
        'w_out': nrm(ks[20], (L, D, D), D ** -0.5),
        'router_w': nrm(ks[21], (L, D, N_EXPERTS), D ** -0.5),
        'router_b': nrm(ks[22], (L, N_EXPERTS), 0.01),
        'w1': nrm(ks[23], (L, N_EXPERTS, D, 2 * D_FF), D ** -0.5),
        'b1': nrm(ks[24], (L, N_EXPERTS, 2 * D_FF), 0.02),
        'w2': nrm(ks[25], (L, N_EXPERTS, D_FF, D), D_FF ** -0.5),
        'b2': nrm(ks[26], (L, N_EXPERTS, D), 0.02),
    }


def reference(x, c, ada_w, ada_b, norm1_g, norm2_g, w_in, gate_b, a_qn_g, a_kn_g, a_lam_q1, a_lam_k1,
              a_lam_q2, a_lam_k2, a_subln_g, c_qn_g, c_kn_g, d_qn_g, d_kn_g, w_branch, w_out,
              router_w, router_b, w1, b1, w2, b2):
    slopes = alibi_slopes()
    cond = jax.nn.silu(c)
    for l in range(DEPTH):
        mod = cond @ ada_w[l] + ada_b[l]
        sh1, sc1, g1, sh2, sc2, g2 = jnp.split(mod[:, None, :], 6, axis=-1)
        h = rms_norm(x, norm1_g[l]) * (1.0 + sc1) + sh1
        x = x + g1 * token_mixer(h, l, w_in, gate_b, a_qn_g, a_kn_g, a_lam_q1, a_lam_k1, a_lam_q2, a_lam_k2,
                                 a_subln_g, c_qn_g, c_kn_g, d_qn_g, d_kn_g, w_branch, w_out, slopes)
        h = rms_norm(x, norm2_g[l]) * (1.0 + sc2) + sh2
        x = x + g2 * moe_ffn(h, l, router_w, router_b, w1, b1, w2, b2)
    return x
```

```python
import functools
import math

import jax
import jax.numpy as jnp
from jax import lax
from jax.experimental import pallas as pl
from jax.experimental.pallas import tpu as pltpu

F32 = jnp.float32
BF16 = jnp.bfloat16
I32 = jnp.int32

A_HEADS, A_DH = 4, 64
B_HEADS, B_DH = 4, 128
C_HEADS, C_DH = 4, 128
D_HEADS, D_DH = 4, 128
IDX_HEADS, IDX_DH = 8, 64
N_BRANCH = 4
MOBA_BLOCK = 256
MOBA_TOPK = 3
DSA_TOPK = 256
TOP_K = 4
SWIGLU_ALPHA = 1.702
SWIGLU_LIMIT = 7.0
N_ALIBI = A_HEADS + C_HEADS + D_HEADS
RMS_EPS = 1e-6

LANE = 128
QKV_COLS = 5960
QKV_PAD = 6144
IQ_BLOCK = 40
DK_BLOCK = 44
DV_BLOCK = 45
IK_BLOCK = 46
NEG_BIG = -1e30
INT_MIN = -(2 ** 31)
MOE_ROWS = 256
VMEM_LIMIT = 56 * 1024 * 1024


def _alibi_slopes():
    return [2.0 ** (-8.0 * (i + 1.0) / N_ALIBI) for i in range(N_ALIBI)]


def _cp(sem, vmem=VMEM_LIMIT):
    return pltpu.CompilerParams(dimension_semantics=sem, vmem_limit_bytes=vmem)


def _dot(a, b):
    return jnp.dot(a, b, preferred_element_type=F32)


def _dot_t(a, b):
    return lax.dot_general(a, b, (((1,), (1,)), ((), ())), preferred_element_type=F32)


def _split(x):
    hi = x.astype(BF16)
    lo = (x - hi.astype(F32)).astype(BF16)
    return hi, lo


def _ada_kernel(c_ref, w_ref, b_ref, o_ref):
    c = c_ref[...]
    cond = c * jax.nn.sigmoid(c)
    ch, cl = _split(cond)
    wh, wl = _split(w_ref[0])
    o_ref[0] = _dot(ch, wh) + _dot(ch, wl) + _dot(cl, wh) + b_ref[0]


def _ada_mod(c, ada_w, ada_b):
    depth, d, n = ada_w.shape
    bsz = c.shape[0]
    bp = -(-bsz // 8) * 8
    cp = jnp.pad(c, ((0, bp - bsz), (0, 0)))
    tn = 1024
    out = pl.pallas_call(
        _ada_kernel,
        grid=(depth, n // tn),
        in_specs=[pl.BlockSpec((bp, d), lambda l, j: (0, 0)),
                  pl.BlockSpec((1, d, tn), lambda l, j: (l, 0, j)),
                  pl.BlockSpec((1, 1, tn), lambda l, j: (l, 0, j))],
        out_specs=pl.BlockSpec((1, bp, tn), lambda l, j: (l, 0, j)),
        out_shape=jax.ShapeDtypeStruct((depth, bp, n), F32),
        compiler_params=_cp(("parallel", "parallel")),
        name="ada_mod",
    )(cp, ada_w, ada_b.reshape(depth, 1, n))
    return out[:, :bsz].reshape(depth, bsz, 6, d)


def _norm_mod(x, g, mod, sh_idx, sc_idx):
    xx = x * x
    ms = jnp.mean(xx, axis=-1, keepdims=True)
    y = x * lax.rsqrt(ms + RMS_EPS) * g
    return y * (1.0 + mod[sc_idx:sc_idx + 1]) + mod[sh_idx:sh_idx + 1]


def _norm1_kernel(x_ref, g_ref, mod_ref, h_ref):
    h = _norm_mod(x_ref[...], g_ref[...], mod_ref[0], 0, 1)
    h_ref[...] = h.astype(BF16)


def _norm1(x2, g, mod, seq):
    t, d = x2.shape
    tm = min(512, seq)
    per_b = seq // tm
    return pl.pallas_call(
        _norm1_kernel,
        grid=(t // tm,),
        in_specs=[pl.BlockSpec((tm, d), lambda i: (i, 0)),
                  pl.BlockSpec((1, d), lambda i: (0, 0)),
                  pl.BlockSpec((1, 6, d), lambda i: (i // per_b, 0, 0))],
        out_specs=pl.BlockSpec((tm, d), lambda i: (i, 0)),
        out_shape=jax.ShapeDtypeStruct((t, d), BF16),
        compiler_params=_cp(("parallel",)),
        name="norm1",
    )(x2, g.reshape(1, d), mod)


def _router_kernel(x_ref, g_ref, mod_ref, rw_ref, rb_ref,
                   hp_ref, idx_ref, gate_ref, rank_ref, cnt_ref, carry_ref, *, n_exp):
    i = pl.program_id(0)
    tm = x_ref.shape[0]
    half = x_ref.shape[1] // 2

    @pl.when(i == 0)
    def _():
        carry_ref[...] = jnp.zeros_like(carry_ref)

    h = _norm_mod(x_ref[...], g_ref[...], mod_ref[0], 3, 4)
    hb = h.astype(BF16)
    lo = lax.bitcast_convert_type(hb[:, :half].astype(F32), I32)
    hi = lax.bitcast_convert_type(hb[:, half:].astype(F32), I32)
    hp_ref[...] = jnp.bitwise_or(lax.shift_right_logical(lo, 16), jnp.bitwise_and(hi, -65536))

    logits = _dot(hb, rw_ref[...]) + rb_ref[...]
    lane = lax.broadcasted_iota(I32, (tm, LANE), 1)
    lane_f = lane.astype(F32)
    cur = jnp.where(lane < n_exp, logits, -jnp.inf)
    idx_out = jnp.zeros((tm, LANE), F32)
    val_out = jnp.zeros((tm, LANE), F32)
    onehot = jnp.zeros((tm, LANE), F32)
    vals = []
    idxs = []
    for k in range(TOP_K):
        m = jnp.max(cur, axis=-1, keepdims=True)
        am = jnp.min(jnp.where(cur == m, lane_f, float(LANE)), axis=-1, keepdims=True)
        hit = lane_f == am
        cur = jnp.where(hit, -jnp.inf, cur)
        onehot = onehot + jnp.where(hit, 1.0, 0.0)
        idx_out = jnp.where(lane == k, am, idx_out)
        vals.append(m)
        idxs.append(am)
    den = sum(jnp.exp(v - vals[0]) for v in vals)
    for k in range(TOP_K):
        val_out = jnp.where(lane == k, jnp.exp(vals[k] - vals[0]) / den, val_out)
    idx_ref[...] = idx_out.astype(I32)
    gate_ref[...] = val_out

    r = lax.broadcasted_iota(I32, (tm, tm), 0)
    c = lax.broadcasted_iota(I32, (tm, tm), 1)
    tri = jnp.where(c < r, 1.0, 0.0).astype(BF16)
    prefix = _dot(tri, onehot.astype(BF16)) + carry_ref[...]
    rank_out = jnp.zeros((tm, LANE), F32)
    for k in range(TOP_K):
        rk = jnp.sum(jnp.where(lane_f == idxs[k], prefix, 0.0), axis=-1, keepdims=True)
        rank_out = jnp.where(lane == k, rk, rank_out)
    rank_ref[...] = rank_out.astype(I32)
    tot = carry_ref[...] + jnp.sum(onehot, axis=0, keepdims=True)
    carry_ref[...] = tot
    cnt_ref[...] = jnp.broadcast_to(tot, cnt_ref.shape).astype(I32)


def _router(x2, g, mod, rw, rb, seq, n_exp):
    t, d = x2.shape
    tm = min(256, seq)
    per_b = seq // tm
    outs = pl.pallas_call(
        functools.partial(_router_kernel, n_exp=n_exp),
        grid=(t // tm,),
        in_specs=[pl.BlockSpec((tm, d), lambda i: (i, 0)),
                  pl.BlockSpec((1, d), lambda i: (0, 0)),
                  pl.BlockSpec((1, 6, d), lambda i: (i // per_b, 0, 0)),
                  pl.BlockSpec((d, LANE), lambda i: (0, 0)),
                  pl.BlockSpec((1, LANE), lambda i: (0, 0))],
        out_specs=[pl.BlockSpec((tm, d // 2), lambda i: (i, 0)),
                   pl.BlockSpec((tm, LANE), lambda i: (i, 0)),
                   pl.BlockSpec((tm, LANE), lambda i: (i, 0)),
                   pl.BlockSpec((tm, LANE), lambda i: (i, 0)),
                   pl.BlockSpec((8, LANE), lambda i: (0, 0))],
        out_shape=[jax.ShapeDtypeStruct((t, d // 2), I32),
                   jax.ShapeDtypeStruct((t, LANE), I32),
                   jax.ShapeDtypeStruct((t, LANE), F32),
                   jax.ShapeDtypeStruct((t, LANE), I32),
                   jax.ShapeDtypeStruct((8, LANE), I32)],
        scratch_shapes=[pltpu.VMEM((1, LANE), F32)],
        compiler_params=_cp(("arbitrary",)),
        name="router",
    )(x2, g.reshape(1, d), mod, rw, rb)
    return outs


def _mm_kernel(a_ref, b_ref, o_ref):
    o_ref[...] = _dot(a_ref[...], b_ref[...]).astype(o_ref.dtype)


def _matmul(a, b, out_dtype, tm, tn, name):
    m, k = a.shape
    n = b.shape[1]
    tm = min(tm, m)
    return pl.pallas_call(
        _mm_kernel,
        grid=(m // tm, n // tn),
        in_specs=[pl.BlockSpec((tm, k), lambda i, j: (i, 0)),
                  pl.BlockSpec((k, tn), lambda i, j: (0, j))],
        out_specs=pl.BlockSpec((tm, tn), lambda i, j: (i, j)),
        out_shape=jax.ShapeDtypeStruct((m, n), out_dtype),
        compiler_params=_cp(("parallel", "parallel")),
        name=name,
    )(a, b)


def _out_proj_kernel(y_ref, w_ref, x_ref, mod_ref, o_ref):
    g = mod_ref[0][2:3]
    o_ref[...] = x_ref[...] + g * _dot(y_ref[...], w_ref[...])


def _out_proj(y, w, x2, mod, seq):
    m, k = y.shape
    n = w.shape[1]
    tm = min(1024, seq)
    tn = 512
    per_b = seq // tm
    return pl.pallas_call(
        _out_proj_kernel,
        grid=(m // tm, n // tn),
        in_specs=[pl.BlockSpec((tm, k), lambda i, j: (i, 0)),
                  pl.BlockSpec((k, tn), lambda i, j: (0, j)),
                  pl.BlockSpec((tm, tn), lambda i, j: (i, j)),
                  pl.BlockSpec((1, 6, tn), lambda i, j: (i // per_b, 0, j))],
        out_specs=pl.BlockSpec((tm, tn), lambda i, j: (i, j)),
        out_shape=jax.ShapeDtypeStruct((m, n), F32),
        compiler_params=_cp(("parallel", "parallel")),
        name="out_proj",
    )(y, w, x2, mod)


def _prep_kernel(p_ref, g_ref, q_ref, w_ref):
    tm = p_ref.shape[0]
    lane = lax.broadcasted_iota(I32, (tm, LANE), 1)
    low = lane < (LANE // 2)

    def blk(c):
        return p_ref[:, c * LANE:(c + 1) * LANE]

    def put(c, v):
        q_ref[:, c * LANE:(c + 1) * LANE] = v.astype(BF16)

    def norm_full(c, row):
        y = blk(c)
        ms = jnp.mean(y * y, axis=-1, keepdims=True)
        put(c, y * lax.rsqrt(ms + RMS_EPS) * g_ref[row:row + 1, :])

    def norm_half(c, row):
        y = blk(c)
        yy = y * y
        s_lo = jnp.sum(jnp.where(low, yy, 0.0), axis=-1, keepdims=True)
        s_hi = jnp.sum(jnp.where(low, 0.0, yy), axis=-1, keepdims=True)
        ms = jnp.where(low, s_lo, s_hi) * (2.0 / LANE)
        put(c, y * lax.rsqrt(ms + RMS_EPS) * g_ref[row:row + 1, :])

    for c in range(QKV_PAD // LANE):
        if c < 4:
            norm_half(c, 0)
        elif c < 8:
            norm_half(c, 1)
        elif 12 <= c < 16:
            put(c, blk(c) * g_ref[6:7, :])
        elif 24 <= c < 28:
            norm_full(c, 2)
        elif 28 <= c < 32:
            norm_full(c, 3)
        elif 36 <= c < 40:
            norm_full(c, 4)
        elif c == DK_BLOCK:
            norm_full(c, 5)
        elif c == IK_BLOCK:
            y = blk(c)
            w_ref[...] = y
            put(c, jnp.where(low, y, pltpu.roll(y, LANE // 2, 1)))
        else:
            put(c, blk(c))


def _prep(p, gains):
    t = p.shape[0]
    tm = 256
    return pl.pallas_call(
        _prep_kernel,
        grid=(t // tm,),
        in_specs=[pl.BlockSpec((tm, QKV_PAD), lambda i: (i, 0)),
                  pl.BlockSpec((8, LANE), lambda i: (0, 0))],
        out_specs=[pl.BlockSpec((tm, QKV_PAD), lambda i: (i, 0)),
                   pl.BlockSpec((tm, LANE), lambda i: (i, 0))],
        out_shape=[jax.ShapeDtypeStruct((t, QKV_PAD), BF16),
                   jax.ShapeDtypeStruct((t, LANE), F32)],
        compiler_params=_cp(("parallel",)),
        name="qk_prep",
    )(p, gains)


def _online_update(s, v, m, l, acc, keep=None):
    m_new = jnp.maximum(m, jnp.max(s, axis=-1, keepdims=True))
    alpha = jnp.exp(m - m_new)
    p = jnp.exp(s - m_new)
    if keep is not None:
        p = jnp.where(keep, p, 0.0)
    l = alpha * l + jnp.sum(p, axis=-1, keepdims=True)
    acc = alpha * acc + _dot(p.astype(BF16), v)
    return m_new, l, acc


def _diff_kernel(sc_ref, q_ref, k_ref, v_ref, g_ref, o_ref, *, tq, slopes):
    i = pl.program_id(1)
    lam = sc_ref[0]
    post = sc_ref[1]
    lane = lax.broadcasted_iota(I32, (tq, LANE), 1)
    low = lane < (LANE // 2)
    row = i * tq + lax.broadcasted_iota(I32, (tq, tq), 0)
    for h in range(A_HEADS):
        cs = slice(h * LANE, (h + 1) * LANE)
        q = q_ref[:, cs]
        zero = jnp.zeros_like(q)
        q12 = jnp.concatenate([jnp.where(low, q, zero), jnp.where(low, zero, q)], axis=0)

        def body(kt, carry, cs=cs, q12=q12, h=h):
            m, l, acc = carry
            off = pl.multiple_of(kt * tq, tq)
            k = k_ref[pl.ds(off, tq), cs]
            v = v_ref[pl.ds(off, tq), cs]
            dist = (row - (off + lax.broadcasted_iota(I32, (tq, tq), 1))).astype(F32)
            bias = jnp.where(dist >= 0, -slopes[h] * dist, NEG_BIG)
            s = _dot_t(q12, k) + jnp.concatenate([bias, bias], axis=0)
            return _online_update(s, v, m, l, acc)

        init = (jnp.full((2 * tq, 1), NEG_BIG, F32), jnp.zeros((2 * tq, 1), F32),
                jnp.zeros((2 * tq, LANE), F32))
        m, l, acc = lax.fori_loop(0, i + 1, body, init)
        o = acc / l
        o = o[:tq] - lam * o[tq:]
        ms = jnp.mean(o * o, axis=-1, keepdims=True)
        o = o * lax.rsqrt(ms + RMS_EPS) * g_ref[...] * post
        o_ref[:, cs] = o.astype(BF16)


def _diff_attention(qkv, scal, subln_g, bsz, seq, slopes):
    tq = min(128, seq)
    nq = seq // tq
    hw = A_HEADS * LANE
    return pl.pallas_call(
        functools.partial(_diff_kernel, tq=tq, slopes=slopes),
        grid=(bsz, nq),
        in_specs=[pl.BlockSpec(memory_space=pltpu.SMEM),
                  pl.BlockSpec((tq, hw), lambda b, i: (b * nq + i, 0)),
                  pl.BlockSpec((seq, hw), lambda b, i: (b, 1)),
                  pl.BlockSpec((seq, hw), lambda b, i: (b, 2)),
                  pl.BlockSpec((1, LANE), lambda b, i: (0, 0))],
        out_specs=pl.BlockSpec((tq, hw), lambda b, i: (b * nq + i, 0)),
        out_shape=jax.ShapeDtypeStruct((bsz * seq, hw), BF16),
        compiler_params=_cp(("parallel", "parallel")),
        name="diff_attn",
    )(scal, qkv, qkv, qkv, subln_g.reshape(1, LANE))


def _stick_kernel(q_ref, k_ref, v_ref, o_ref, *, tq):
    i = pl.program_id(1)
    row = i * tq + lax.broadcasted_iota(I32, (tq, tq), 0)
    r = lax.broadcasted_iota(I32, (tq, tq), 0)
    c = lax.broadcasted_iota(I32, (tq, tq), 1)
    upper = jnp.where(r > c, 1.0, 0.0).astype(BF16)
    for h in range(B_HEADS):
        cs = slice(h * LANE, (h + 1) * LANE)
        q = q_ref[:, cs]

        def body(j, carry, cs=cs, q=q):
            tail, acc = carry
            kt = i - j
            off = pl.multiple_of(kt * tq, tq)
            k = k_ref[pl.ds(off, tq), cs]
            v = v_ref[pl.ds(off, tq), cs]
            mask = (off + lax.broadcasted_iota(I32, (tq, tq), 1)) < row
            z = _dot_t(q, k)
            ls = jnp.minimum(z, 0.0) - jnp.log1p(jnp.exp(-jnp.abs(z)))
            l1 = jnp.where(mask, ls - z, 0.0)
            hi, lo = _split(l1)
            after = _dot(hi, upper) + _dot(lo, upper) + tail
            w = jnp.where(mask, jnp.exp(ls + after), 0.0)
            acc = acc + _dot(w.astype(BF16), v)
            tail = tail + jnp.sum(l1, axis=-1, keepdims=True)
            return tail, acc

        init = (jnp.zeros((tq, 1), F32), jnp.zeros((tq, LANE), F32))
        _, acc = lax.fori_loop(0, i + 1, body, init)
        o_ref[:, cs] = acc.astype(BF16)


def _stick_attention(qkv, bsz, seq):
    tq = min(128, seq)
    nq = seq // tq
    hw = B_HEADS * LANE
    return pl.pallas_call(
        functools.partial(_stick_kernel, tq=tq),
        grid=(bsz, nq),
        in_specs=[pl.BlockSpec((tq, hw), lambda b, i: (b * nq + i, 3)),
                  pl.BlockSpec((seq, hw), lambda b, i: (b, 4)),
                  pl.BlockSpec((seq, hw), lambda b, i: (b, 5))],
        out_specs=pl.BlockSpec((tq, hw), lambda b, i: (b * nq + i, 0)),
        out_shape=jax.ShapeDtypeStruct((bsz * seq, hw), BF16),
        compiler_params=_cp(("parallel", "parallel")),
        name="stick_attn",
    )(qkv, qkv, qkv)


def _moba_kernel(q_ref, k_ref, v_ref, o_ref, *, tq, n_kb, n_sel, slopes):
    i = pl.program_id(1)
    seq = k_ref.shape[0]
    lane = lax.broadcasted_iota(I32, (tq, LANE), 1)
    row = i * tq + lax.broadcasted_iota(I32, (tq, tq), 0)
    col0 = lax.broadcasted_iota(I32, (tq, tq), 1)
    br = lax.broadcasted_iota(I32, (LANE, seq), 0)
    bc = lax.broadcasted_iota(I32, (LANE, seq), 1)
    lo_edge = br * tq
    avg = jnp.where((bc >= lo_edge) & (bc < lo_edge + tq), 1.0 / tq, 0.0).astype(BF16)
    for h in range(C_HEADS):
        cs = slice(h * LANE, (h + 1) * LANE)
        q = q_ref[:, cs]
        k_mean = _dot(avg, k_ref[:, cs]).astype(BF16)
        gate = jnp.where(lane < i, _dot_t(q, k_mean), -jnp.inf)
        rank = jnp.zeros((tq, LANE), F32)
        for mth in range(n_kb):
            gm = gate[:, mth:mth + 1]
            beats = (gm > gate) | ((gm == gate) & (lane > mth))
            rank = rank + jnp.where(beats, 1.0, 0.0)
        sel = jnp.where((rank < n_sel) & (lane < i), 1.0, 0.0)

        off = pl.multiple_of(i * tq, tq)
        dist = (row - (off + col0)).astype(F32)
        bias = jnp.where(dist >= 0, -slopes[h] * dist, NEG_BIG)
        s = _dot_t(q, k_ref[pl.ds(off, tq), cs]) + bias
        init = _online_update(s, v_ref[pl.ds(off, tq), cs],
                              jnp.full((tq, 1), NEG_BIG, F32), jnp.zeros((tq, 1), F32),
                              jnp.zeros((tq, LANE), F32))

        def body(kt, carry, cs=cs, q=q, sel=sel, h=h):
            m, l, acc = carry
            off = pl.multiple_of(kt * tq, tq)
            flag = jnp.max(jnp.where(lane == kt, sel, 0.0), axis=-1, keepdims=True)
            keep = jnp.broadcast_to(flag, (tq, tq)) > 0.0
            dist = (row - (off + col0)).astype(F32)
            s = _dot_t(q, k_ref[pl.ds(off, tq), cs]) - slopes[h] * dist
            s = jnp.where(keep, s, NEG_BIG)
            return _online_update(s, v_ref[pl.ds(off, tq), cs], m, l, acc, keep=keep)

        m, l, acc = lax.fori_loop(0, i, body, init)
        o_ref[:, cs] = (acc / l).astype(BF16)


def _moba_attention(qkv, bsz, seq, slopes):
    tq = MOBA_BLOCK
    assert seq % tq == 0
    n_kb = seq // tq
    assert n_kb <= LANE
    n_sel = min(MOBA_TOPK, n_kb - 1)
    hw = C_HEADS * LANE
    return pl.pallas_call(
        functools.partial(_moba_kernel, tq=tq, n_kb=n_kb, n_sel=n_sel, slopes=slopes),
        grid=(bsz, n_kb),
        in_specs=[pl.BlockSpec((tq, hw), lambda b, i: (b * n_kb + i, 6)),
                  pl.BlockSpec((seq, hw), lambda b, i: (b, 7)),
                  pl.BlockSpec((seq, hw), lambda b, i: (b, 8))],
        out_specs=pl.BlockSpec((tq, hw), lambda b, i: (b * n_kb + i, 0)),
        out_shape=jax.ShapeDtypeStruct((bsz * seq, hw), BF16),
        compiler_params=_cp(("parallel", "parallel")),
        name="moba_attn",
    )(qkv, qkv, qkv)


def _dsa_kernel(q_ref, k_ref, v_ref, iq_ref, ki_ref, w_ref, o_ref, keys_ref, cut_ref,
                *, tq, n_keep, idx_bits, slopes):
    i = pl.program_id(1)
    n_kt = i + 1
    lane = lax.broadcasted_iota(I32, (tq, LANE), 1)
    low = lane < (LANE // 2)
    row = i * tq + lax.broadcasted_iota(I32, (tq, tq), 0)
    col0 = lax.broadcasted_iota(I32, (tq, tq), 1)
    w_scale = (IDX_DH ** -0.5) * (IDX_HEADS ** -0.5)
    wv = w_ref[...] * w_scale
    w_cols = [wv[:, LANE // 2 + j:LANE // 2 + j + 1] for j in range(IDX_HEADS)]

    def score_body(kt, _):
        off = pl.multiple_of(kt * tq, tq)
        kk = ki_ref[pl.ds(off, tq), :]
        sc = jnp.zeros((tq, tq), F32)
        for p in range(IDX_HEADS // 2):
            qp = iq_ref[:, p * LANE:(p + 1) * LANE]
            zero = jnp.zeros_like(qp)
            le = _dot_t(jnp.where(low, qp, zero), kk)
            lo = _dot_t(jnp.where(low, zero, qp), kk)
            sc = sc + jnp.maximum(le, 0.0) * w_cols[2 * p] + jnp.maximum(lo, 0.0) * w_cols[2 * p + 1]
        sc = jnp.where(sc == 0.0, 0.0, sc)
        bits = lax.bitcast_convert_type(sc, I32)
        key = jnp.bitwise_xor(bits, jnp.bitwise_and(lax.shift_right_arithmetic(bits, 31), 0x7FFFFFFF))
        keys_ref[:, pl.ds(off, tq)] = jnp.where(off + col0 <= row, key, INT_MIN)
        return 0

    lax.fori_loop(0, n_kt, score_body, 0)

    def count(pred):
        def body(kt, acc):
            off = pl.multiple_of(kt * tq, tq)
            return acc + jnp.where(pred(keys_ref[:, pl.ds(off, tq)], off + col0), 1.0, 0.0)
        acc = lax.fori_loop(0, n_kt, body, jnp.zeros((tq, tq), F32))
        return jnp.sum(acc, axis=-1, keepdims=True)

    keep_f = float(n_keep)
    zero_t = jnp.zeros((tq, 1), I32)
    c0 = count(lambda kc, col: kc >= zero_t)
    thr = jnp.where(c0 >= keep_f, 0, INT_MIN).astype(I32)

    def bit_body(b, thr):
        cand = jnp.bitwise_or(thr, lax.shift_left(jnp.int32(1), 30 - b))
        cnt = count(lambda kc, col: kc >= cand)
        return jnp.where(cnt >= keep_f, cand, thr)

    thr = lax.fori_loop(0, 31, bit_body, thr)

    n_ge = count(lambda kc, col: kc >= thr)
    n_gt = count(lambda kc, col: kc > thr)
    need = (n_ge > keep_f) & (thr > INT_MIN)
    quota = keep_f - n_gt
    big = jnp.int32(2 ** 30)
    cut_ref[...] = jnp.full((tq, 1), big, I32)

    @pl.when(jnp.max(jnp.where(need, 1.0, 0.0)) > 0.0)
    def _():
        def tie_body(b, cut):
            cand = jnp.bitwise_or(cut, lax.shift_left(jnp.int32(1), idx_bits - 1 - b))
            cnt = count(lambda kc, col: (kc == thr) & (col < cand))
            return jnp.where(cnt <= quota, cand, cut)
        cut = lax.fori_loop(0, idx_bits, tie_body, jnp.zeros((tq, 1), I32))
        cut_ref[...] = jnp.where(need, cut, big)

    cut = cut_ref[...]

    def flag_body(kt, _):
        off = pl.multiple_of(kt * tq, tq)
        kc = keys_ref[:, pl.ds(off, tq)]
        col = off + col0
        sel = ((kc > thr) | ((kc == thr) & (col < cut))) & (col <= row)
        keys_ref[:, pl.ds(off, tq)] = jnp.where(sel, 1, 0).astype(I32)
        return 0

    lax.fori_loop(0, n_kt, flag_body, 0)

    for h in range(D_HEADS):
        cs = slice(h * LANE, (h + 1) * LANE)
        q = q_ref[:, cs]

        def body(kt, carry, q=q, h=h):
            m, l, acc = carry
            off = pl.multiple_of(kt * tq, tq)
            keep = keys_ref[:, pl.ds(off, tq)] > 0
            dist = (row - (off + col0)).astype(F32)
            s = _dot_t(q, k_ref[pl.ds(off, tq), :]) - slopes[h] * dist
            s = jnp.where(keep, s, NEG_BIG)
            return _online_update(s, v_ref[pl.ds(off, tq), :], m, l, acc, keep=keep)

        init = (jnp.full((tq, 1), NEG_BIG, F32), jnp.zeros((tq, 1), F32), jnp.zeros((tq, LANE), F32))
        m, l, acc = lax.fori_loop(0, n_kt, body, init)
        o_ref[:, cs] = (acc / l).astype(BF16)


def _dsa_attention(qkv, wraw, bsz, seq, slopes):
    tq = min(128, seq)
    nq = seq // tq
    n_keep = min(DSA_TOPK, seq // 4)
    idx_bits = int(math.ceil(math.log2(seq))) + 1
    hw = D_HEADS * LANE
    return pl.pallas_call(
        functools.partial(_dsa_kernel, tq=tq, n_keep=n_keep, idx_bits=idx_bits, slopes=slopes),
        grid=(bsz, nq),
        in_specs=[pl.BlockSpec((tq, hw), lambda b, i: (b * nq + i, 9)),
                  pl.BlockSpec((seq, LANE), lambda b, i: (b, DK_BLOCK)),
                  pl.BlockSpec((seq, LANE), lambda b, i: (b, DV_BLOCK)),
                  pl.BlockSpec((tq, IDX_HEADS * IDX_DH), lambda b, i: (b * nq + i, IQ_BLOCK * LANE // (IDX_HEADS * IDX_DH))),
                  pl.BlockSpec((seq, LANE), lambda b, i: (b, IK_BLOCK)),
                  pl.BlockSpec((tq, LANE), lambda b, i: (b * nq + i, 0))],
        out_specs=pl.BlockSpec((tq, hw), lambda b, i: (b * nq + i, 0)),
        out_shape=jax.ShapeDtypeStruct((bsz * seq, hw), BF16),
        scratch_shapes=[pltpu.VMEM((tq, seq), I32), pltpu.VMEM((tq, 1), I32)],
        compiler_params=_cp(("parallel", "parallel")),
        name="dsa_attn",
    )(qkv, qkv, qkv, qkv, qkv, wraw)


def _merge_kernel(a_ref, b_ref, c_ref, d_ref, wb_ref, g0, g1, g2, g3, gb_ref, y_ref):
    acc = None
    for n, (br, gl) in enumerate(zip((a_ref, b_ref, c_ref, d_ref), (g0, g1, g2, g3))):
        up = _dot(br[...], wb_ref[n])
        gate = jax.nn.sigmoid(gl[...].astype(F32) + gb_ref[n])
        acc = gate * up if acc is None else acc + gate * up
    y_ref[...] = acc.astype(BF16)


def _merge(branches, wb, gl, gate_b):
    t, bw = branches[0].shape
    d = wb.shape[2]
    tm = min(1024, t)
    tn = 512
    nj = d // tn
    br_spec = pl.BlockSpec((tm, bw), lambda j, i: (i, 0))
    gl_specs = [pl.BlockSpec((tm, tn), functools.partial(lambda j, i, n: (i, n * nj + j), n=n))
                for n in range(N_BRANCH)]
    return pl.pallas_call(
        _merge_kernel,
        grid=(nj, t // tm),
        in_specs=[br_spec] * 4 + [pl.BlockSpec((N_BRANCH, bw, tn), lambda j, i: (0, 0, j))] + gl_specs
                 + [pl.BlockSpec((N_BRANCH, 1, tn), lambda j, i: (0, 0, j))],
        out_specs=pl.BlockSpec((tm, tn), lambda j, i: (i, j)),
        out_shape=jax.ShapeDtypeStruct((t, d), BF16),
        compiler_params=_cp(("parallel", "parallel")),
        name="merge",
    )(*branches, wb, gl, gl, gl, gl, gate_b.reshape(N_BRANCH, 1, d))


def _gather_rows_kernel(tok_ref, src_ref, dst_ref, sem, *, rows):
    base = pl.program_id(0) * rows

    def issue(r, _):
        pltpu.make_async_copy(src_ref.at[pl.ds(tok_ref[base + r], 1)],
                              dst_ref.at[pl.ds(base + r, 1)], sem).start()
        return 0

    lax.fori_loop(0, rows, issue, 0)

    def drain(r, _):
        pltpu.make_async_copy(src_ref.at[pl.ds(0, 1)], dst_ref.at[pl.ds(base + r, 1)], sem).wait()
        return 0

    lax.fori_loop(0, rows, drain, 0)


def _gather_rows(src, row_tok):
    n_rows = row_tok.shape[0]
    rows = MOE_ROWS
    return pl.pallas_call(
        functools.partial(_gather_rows_kernel, rows=rows),
        grid_spec=pltpu.PrefetchScalarGridSpec(
            num_scalar_prefetch=1,
            grid=(n_rows // rows,),
            in_specs=[pl.BlockSpec(memory_space=pl.ANY)],
            out_specs=pl.BlockSpec(memory_space=pl.ANY),
            scratch_shapes=[pltpu.SemaphoreType.DMA]),
        out_shape=jax.ShapeDtypeStruct((n_rows, src.shape[1]), src.dtype),
        compiler_params=_cp(("arbitrary",)),
        name="moe_gather",
    )(row_tok, src)


def _unpack_rows(xp):
    lo = lax.bitcast_convert_type(lax.shift_left(xp, 16), F32).astype(BF16)
    hi = lax.bitcast_convert_type(jnp.bitwise_and(xp, -65536), F32).astype(BF16)
    return lo, hi


def _expert_up_kernel(be_ref, nu_ref, xp_ref, wg_ref, wl_ref, bg_ref, bl_ref, act_ref):
    i = pl.program_id(1)

    @pl.when(i < nu_ref[0])
    def _():
        half = xp_ref.shape[1]
        lo, hi = _unpack_rows(xp_ref[...])
        glu = _dot(lo, wg_ref[0, :half, :]) + _dot(hi, wg_ref[0, half:, :]) + bg_ref[0]
        lin = _dot(lo, wl_ref[0, :half, :]) + _dot(hi, wl_ref[0, half:, :]) + bl_ref[0]
        glu = jnp.minimum(glu, SWIGLU_LIMIT)
        lin = jnp.clip(lin, -SWIGLU_LIMIT, SWIGLU_LIMIT)
        act = glu * jax.nn.sigmoid(SWIGLU_ALPHA * glu) * (lin + 1.0)
        act_ref[...] = act.astype(BF16)

    @pl.when(i >= nu_ref[0])
    def _():
        act_ref[...] = jnp.zeros_like(act_ref)


def _expert_up(block_exp, n_used, xs, w1, b1, d_ff):
    n_rows, half = xs.shape
    d = 2 * half
    mb = MOE_ROWS
    tf = 512
    nf = d_ff // tf
    n_blocks = n_rows // mb

    def rows(j, i, be, nu):
        return (jnp.minimum(i, nu[0] - 1), 0)

    return pl.pallas_call(
        _expert_up_kernel,
        grid_spec=pltpu.PrefetchScalarGridSpec(
            num_scalar_prefetch=2,
            grid=(nf, n_blocks),
            in_specs=[pl.BlockSpec((mb, half), rows),
                      pl.BlockSpec((1, d, tf), lambda j, i, be, nu: (be[i], 0, j)),
                      pl.BlockSpec((1, d, tf), lambda j, i, be, nu: (be[i], 0, nf + j)),
                      pl.BlockSpec((1, 1, tf), lambda j, i, be, nu: (be[i], 0, j)),
                      pl.BlockSpec((1, 1, tf), lambda j, i, be, nu: (be[i], 0, nf + j))],
            out_specs=pl.BlockSpec((mb, tf), lambda j, i, be, nu: (i, j))),
        out_shape=jax.ShapeDtypeStruct((n_rows, d_ff), BF16),
        compiler_params=_cp(("arbitrary", "arbitrary")),
        name="expert_up",
    )(block_exp, n_used, xs, w1, w1, b1, b1)


def _expert_down_kernel(be_ref, nu_ref, a_ref, w_ref, b_ref, y_ref):
    i = pl.program_id(1)

    @pl.when(i < nu_ref[0])
    def _():
        y_ref[...] = _dot(a_ref[...], w_ref[0]) + b_ref[0]

    @pl.when(i >= nu_ref[0])
    def _():
        y_ref[...] = jnp.zeros_like(y_ref)


def _expert_down(block_exp, n_used, act, w2, b2):
    n_rows, d_ff = act.shape
    d = w2.shape[2]
    mb = MOE_ROWS
    tn = 1024
    nn = d // tn
    n_blocks = n_rows // mb
    return pl.pallas_call(
        _expert_down_kernel,
        grid_spec=pltpu.PrefetchScalarGridSpec(
            num_scalar_prefetch=2,
            grid=(nn, n_blocks),
            in_specs=[pl.BlockSpec((mb, d_ff), lambda j, i, be, nu: (jnp.minimum(i, nu[0] - 1), 0)),
                      pl.BlockSpec((1, d_ff, tn), lambda j, i, be, nu: (be[i], 0, j)),
                      pl.BlockSpec((1, 1, tn), lambda j, i, be, nu: (be[i], 0, j))],
            out_specs=pl.BlockSpec((mb, tn), lambda j, i, be, nu: (i, j))),
        out_shape=jax.ShapeDtypeStruct((n_rows, d), F32),
        compiler_params=_cp(("arbitrary", "arbitrary")),
        name="expert_down",
    )(block_exp, n_used, act, w2, b2)


def _combine_kernel(dest_ref, y_ref, x_ref, gate_ref, mod_ref, o_ref, buf, sem, *, tm):
    base = pl.program_id(0) * tm * TOP_K

    def issue(r, _):
        for k in range(TOP_K):
            pltpu.make_async_copy(y_ref.at[pl.ds(dest_ref[base + r * TOP_K + k], 1)],
                                  buf.at[k, pl.ds(r, 1)], sem).start()
        return 0

    lax.fori_loop(0, tm, issue, 0)

    def drain(r, _):
        for k in range(TOP_K):
            pltpu.make_async_copy(y_ref.at[pl.ds(0, 1)], buf.at[k, pl.ds(r, 1)], sem).wait()
        return 0

    lax.fori_loop(0, tm, drain, 0)
    gate = gate_ref[...]
    acc = gate[:, 0:1] * buf[0]
    for k in range(1, TOP_K):
        acc = acc + gate[:, k:k + 1] * buf[k]
    o_ref[...] = x_ref[...] + mod_ref[0][5:6] * acc


def _combine(dest, y, x2, gate, mod, seq):
    t, d = x2.shape
    tm = min(128, seq)
    per_b = seq // tm
    return pl.pallas_call(
        functools.partial(_combine_kernel, tm=tm),
        grid_spec=pltpu.PrefetchScalarGridSpec(
            num_scalar_prefetch=1,
            grid=(t // tm,),
            in_specs=[pl.BlockSpec(memory_space=pl.ANY),
                      pl.BlockSpec((tm, d), lambda i, dst: (i, 0)),
                      pl.BlockSpec((tm, LANE), lambda i, dst: (i, 0)),
                      pl.BlockSpec((1, 6, d), lambda i, dst: (i // per_b, 0, 0))],
            out_specs=pl.BlockSpec((tm, d), lambda i, dst: (i, 0)),
            scratch_shapes=[pltpu.VMEM((TOP_K, tm, d), F32), pltpu.SemaphoreType.DMA]),
        out_shape=jax.ShapeDtypeStruct((t, d), F32),
        compiler_params=_cp(("arbitrary",)),
        name="moe_combine",
    )(dest, y, x2, gate, mod)


def _moe(x2, g, mod, rw, rb, w1, b1, w2, b2, seq):
    t, d = x2.shape
    n_exp, _, two_f = w1.shape
    d_ff = two_f // 2
    rw_p = jnp.pad(rw, ((0, 0), (0, LANE - n_exp))).astype(BF16)
    rb_p = jnp.pad(rb, (0, LANE - n_exp)).reshape(1, LANE)
    hp, idx, gate, rank, cnt = _router(x2, g, mod, rw_p, rb_p, seq, n_exp)

    mb = MOE_ROWS
    counts = cnt[0, :n_exp]
    padded = (counts + mb - 1) // mb * mb
    pend = jnp.cumsum(padded)
    pstart = pend - padded
    top_idx = idx[:, :TOP_K]
    dest = (pstart[top_idx] + rank[:, :TOP_K]).reshape(-1).astype(I32)
    n_assign = t * TOP_K
    n_blocks = -(-(n_assign + n_exp * (mb - 1)) // mb)
    n_rows = n_blocks * mb
    tok = (jnp.arange(n_assign, dtype=I32) // TOP_K)
    row_tok = jnp.zeros((n_rows,), I32).at[dest].set(tok)
    block_start = jnp.arange(n_blocks, dtype=I32) * mb
    block_exp = jnp.minimum(jnp.sum(pend[None, :] <= block_start[:, None], axis=1), n_exp - 1).astype(I32)
    n_used = (pend[-1:] // mb).astype(I32)

    xs = _gather_rows(hp, row_tok)
    act = _expert_up(block_exp, n_used, xs, w1.astype(BF16), b1.reshape(n_exp, 1, two_f), d_ff)
    y = _expert_down(block_exp, n_used, act, w2.astype(BF16), b2.reshape(n_exp, 1, d))
    return _combine(dest, y, x2, gate, mod, seq)


def _token_mixer(x2, l, mod, bsz, seq, norm1_g, w_in, gate_b, a_qn_g, a_kn_g, a_lam_q1, a_lam_k1,
                 a_lam_q2, a_lam_k2, a_subln_g, c_qn_g, c_kn_g, d_qn_g, d_kn_g, w_branch, w_out):
    d = x2.shape[1]
    slopes = _alibi_slopes()
    h = _norm1(x2, norm1_g, mod, seq)
    dq_end = DK_BLOCK * LANE - IDX_HEADS * IDX_DH
    iq_lo = dq_end + 2 * D_DH
    iq_hi = iq_lo + IDX_HEADS * IDX_DH
    w_qkv = jnp.concatenate([w_in[:, :dq_end], w_in[:, iq_lo:iq_hi], w_in[:, dq_end:iq_lo],
                             w_in[:, iq_hi:QKV_COLS], jnp.zeros((d, QKV_PAD - QKV_COLS), F32)],
                            axis=1).astype(BF16)
    w_gl = w_in[:, QKV_COLS:].astype(BF16)
    p = _matmul(h, w_qkv, F32, 1024, 512, "in_proj_qkv")
    gl = _matmul(h, w_gl, BF16, 1024, 512, "in_proj_gate")

    gains = jnp.stack([
        jnp.tile(a_qn_g, 2) * (A_DH ** -0.5),
        jnp.tile(a_kn_g, 2),
        c_qn_g * (C_DH ** -0.5),
        c_kn_g,
        d_qn_g * (D_DH ** -0.5),
        d_kn_g,
        jnp.full((LANE,), B_DH ** -0.5, F32),
        jnp.ones((LANE,), F32)]).astype(F32)
    qkv, wraw = _prep(p, gains)

    lambda_init = 0.8 - 0.6 * math.exp(-0.3 * l)
    lam = (jnp.exp(jnp.sum(a_lam_q1 * a_lam_k1)) - jnp.exp(jnp.sum(a_lam_q2 * a_lam_k2)) + lambda_init)
    scal = jnp.stack([lam, jnp.float32(1.0 - lambda_init)]).astype(F32)

    oa = _diff_attention(qkv, scal, a_subln_g, bsz, seq, slopes[0::3])
    ob = _stick_attention(qkv, bsz, seq)
    oc = _moba_attention(qkv, bsz, seq, slopes[1::3])
    od = _dsa_attention(qkv, wraw, bsz, seq, slopes[2::3])

    y = _merge([oa, ob, oc, od], w_branch.astype(BF16), gl, gate_b)
    return _out_proj(y, w_out.astype(BF16), x2, mod, seq)


def kernel(x, c, ada_w, ada_b, norm1_g, norm2_g, w_in, gate_b, a_qn_g, a_kn_g, a_lam_q1, a_lam_k1,
           a_lam_q2, a_lam_k2, a_subln_g, c_qn_g, c_kn_g, d_qn_g, d_kn_g, w_branch, w_out,
           router_w, router_b, w1, b1, w2, b2):
    bsz, seq, d = x.shape
    depth = ada_w.shape[0]
    mods = _ada_mod(c, ada_w, ada_b)
    x2 = x.reshape(bsz * seq, d)
    for l in range(depth):
        mod = mods[l]
        x2 = _token_mixer(x2, l, mod, bsz, seq, norm1_g[l], w_in[l], gate_b[l], a_qn_g[l], a_kn_g[l],
                          a_lam_q1[l], a_lam_k1[l], a_lam_q2[l], a_lam_k2[l], a_subln_g[l],
                          c_qn_g[l], c_kn_g[l], d_qn_g[l], d_kn_g[l], w_branch[l], w_out[l])
        x2 = _moe(x2, norm2_g[l], mod, router_w[l], router_b[l], w1[l], b1[l], w2[l], b2[l], seq)
    return x2.reshape(bsz, seq, d)
```

```python
import functools
import math

import jax
import jax.numpy as jnp
from jax import lax
from jax.experimental import pallas as pl
from jax.experimental.pallas import tpu as pltpu

F32 = jnp.float32
BF16 = jnp.bfloat16
I32 = jnp.int32

A_HEADS, A_DH = 4, 64
B_HEADS, B_DH = 4, 128
C_HEADS, C_DH = 4, 128
D_HEADS, D_DH = 4, 128
IDX_HEADS, IDX_DH = 8, 64
N_BRANCH = 4
MOBA_BLOCK = 256
MOBA_TOPK = 3
DSA_TOPK = 256
TOP_K = 4
SWIGLU_ALPHA = 1.702
SWIGLU_LIMIT = 7.0
N_ALIBI = A_HEADS + C_HEADS + D_HEADS
RMS_EPS = 1e-6

LANE = 128
QKV_COLS = 5960
QKV_PAD = 6144
IQ_BLOCK = 40
DK_BLOCK = 44
DV_BLOCK = 45
IK_BLOCK = 46
NEG_BIG = -1e30
INT_MIN = -(2 ** 31)
MOE_ROWS = 256
VMEM_LIMIT = 56 * 1024 * 1024


def _alibi_slopes():
    return [2.0 ** (-8.0 * (i + 1.0) / N_ALIBI) for i in range(N_ALIBI)]


def _cp(sem, vmem=VMEM_LIMIT):
    return pltpu.CompilerParams(dimension_semantics=sem, vmem_limit_bytes=vmem)


def _dot(a, b):
    return jnp.dot(a, b, preferred_element_type=F32)


def _dot_t(a, b):
    return lax.dot_general(a, b, (((1,), (1,)), ((), ())), preferred_element_type=F32)


def _split(x):
    hi = x.astype(BF16)
    lo = (x - hi.astype(F32)).astype(BF16)
    return hi, lo


def _ada_kernel(c_ref, w_ref, b_ref, o_ref):
    c = c_ref[...]
    cond = c * jax.nn.sigmoid(c)
    ch, cl = _split(cond)
    wh, wl = _split(w_ref[0])
    o_ref[0] = _dot(ch, wh) + _dot(ch, wl) + _dot(cl, wh) + b_ref[0]


def _ada_mod(c, ada_w, ada_b):
    depth, d, n = ada_w.shape
    bsz = c.shape[0]
    bp = -(-bsz // 8) * 8
    cp = jnp.pad(c, ((0, bp - bsz), (0, 0)))
    tn = 1024
    out = pl.pallas_call(
        _ada_kernel,
        grid=(depth, n // tn),
        in_specs=[pl.BlockSpec((bp, d), lambda l, j: (0, 0)),
                  pl.BlockSpec((1, d, tn), lambda l, j: (l, 0, j)),
                  pl.BlockSpec((1, 1, tn), lambda l, j: (l, 0, j))],
        out_specs=pl.BlockSpec((1, bp, tn), lambda l, j: (l, 0, j)),
        out_shape=jax.ShapeDtypeStruct((depth, bp, n), F32),
        compiler_params=_cp(("parallel", "parallel")),
        name="ada_mod",
    )(cp, ada_w, ada_b.reshape(depth, 1, n))
    return out[:, :bsz].reshape(depth, bsz, 6, d)


def _norm_mod(x, g, mod, sh_idx, sc_idx):
    xx = x * x
    ms = jnp.mean(xx, axis=-1, keepdims=True)
    y = x * lax.rsqrt(ms + RMS_EPS) * g
    return y * (1.0 + mod[sc_idx:sc_idx + 1]) + mod[sh_idx:sh_idx + 1]


def _norm1_kernel(x_ref, g_ref, mod_ref, h_ref):
    h = _norm_mod(x_ref[...], g_ref[...], mod_ref[0], 0, 1)
    h_ref[...] = h.astype(BF16)


def _norm1(x2, g, mod, seq):
    t, d = x2.shape
    tm = min(512, seq)
    per_b = seq // tm
    return pl.pallas_call(
        _norm1_kernel,
        grid=(t // tm,),
        in_specs=[pl.BlockSpec((tm, d), lambda i: (i, 0)),
                  pl.BlockSpec((1, d), lambda i: (0, 0)),
                  pl.BlockSpec((1, 6, d), lambda i: (i // per_b, 0, 0))],
        out_specs=pl.BlockSpec((tm, d), lambda i: (i, 0)),
        out_shape=jax.ShapeDtypeStruct((t, d), BF16),
        compiler_params=_cp(("parallel",)),
        name="norm1",
    )(x2, g.reshape(1, d), mod)


def _router_kernel(x_ref, g_ref, mod_ref, rw_ref, rb_ref,
                   hp_ref, idx_ref, gate_ref, rank_ref, cnt_ref, carry_ref, *, n_exp):
    i = pl.program_id(0)
    tm = x_ref.shape[0]
    half = x_ref.shape[1] // 2

    @pl.when(i == 0)
    def _():
        carry_ref[...] = jnp.zeros_like(carry_ref)

    h = _norm_mod(x_ref[...], g_ref[...], mod_ref[0], 3, 4)
    hb = h.astype(BF16)
    lo = lax.bitcast_convert_type(hb[:, :half].astype(F32), I32)
    hi = lax.bitcast_convert_type(hb[:, half:].astype(F32), I32)
    hp_ref[...] = jnp.bitwise_or(lax.shift_right_logical(lo, 16), jnp.bitwise_and(hi, -65536))

    logits = _dot(hb, rw_ref[...]) + rb_ref[...]
    lane = lax.broadcasted_iota(I32, (tm, LANE), 1)
    lane_f = lane.astype(F32)
    cur = jnp.where(lane < n_exp, logits, -jnp.inf)
    idx_out = jnp.zeros((tm, LANE), F32)
    val_out = jnp.zeros((tm, LANE), F32)
    onehot = jnp.zeros((tm, LANE), F32)
    vals = []
    idxs = []
    for k in range(TOP_K):
        m = jnp.max(cur, axis=-1, keepdims=True)
        am = jnp.min(jnp.where(cur == m, lane_f, float(LANE)), axis=-1, keepdims=True)
        hit = lane_f == am
        cur = jnp.where(hit, -jnp.inf, cur)
        onehot = onehot + jnp.where(hit, 1.0, 0.0)
        idx_out = jnp.where(lane == k, am, idx_out)
        vals.append(m)
        idxs.append(am)
    den = sum(jnp.exp(v - vals[0]) for v in vals)
    for k in range(TOP_K):
        val_out = jnp.where(lane == k, jnp.exp(vals[k] - vals[0]) / den, val_out)
    idx_ref[...] = idx_out.astype(I32)
    gate_ref[...] = val_out

    r = lax.broadcasted_iota(I32, (tm, tm), 0)
    c = lax.broadcasted_iota(I32, (tm, tm), 1)
    tri = jnp.where(c < r, 1.0, 0.0).astype(BF16)
    prefix = _dot(tri, onehot.astype(BF16)) + carry_ref[...]
    rank_out = jnp.zeros((tm, LANE), F32)
    for k in range(TOP_K):
        rk = jnp.sum(jnp.where(lane_f == idxs[k], prefix, 0.0), axis=-1, keepdims=True)
        rank_out = jnp.where(lane == k, rk, rank_out)
    rank_ref[...] = rank_out.astype(I32)
    tot = carry_ref[...] + jnp.sum(onehot, axis=0, keepdims=True)
    carry_ref[...] = tot
    cnt_ref[...] = jnp.broadcast_to(tot, cnt_ref.shape).astype(I32)


def _router(x2, g, mod, rw, rb, seq, n_exp):
    t, d = x2.shape
    tm = min(256, seq)
    per_b = seq // tm
    outs = pl.pallas_call(
        functools.partial(_router_kernel, n_exp=n_exp),
        grid=(t // tm,),
        in_specs=[pl.BlockSpec((tm, d), lambda i: (i, 0)),
                  pl.BlockSpec((1, d), lambda i: (0, 0)),
                  pl.BlockSpec((1, 6, d), lambda i: (i // per_b, 0, 0)),
                  pl.BlockSpec((d, LANE), lambda i: (0, 0)),
                  pl.BlockSpec((1, LANE), lambda i: (0, 0))],
        out_specs=[pl.BlockSpec((tm, d // 2), lambda i: (i, 0)),
                   pl.BlockSpec((tm, LANE), lambda i: (i, 0)),
                   pl.BlockSpec((tm, LANE), lambda i: (i, 0)),
                   pl.BlockSpec((tm, LANE), lambda i: (i, 0)),
                   pl.BlockSpec((8, LANE), lambda i: (0, 0))],
        out_shape=[jax.ShapeDtypeStruct((t, d // 2), I32),
                   jax.ShapeDtypeStruct((t, LANE), I32),
                   jax.ShapeDtypeStruct((t, LANE), F32),
                   jax.ShapeDtypeStruct((t, LANE), I32),
                   jax.ShapeDtypeStruct((8, LANE), I32)],
        scratch_shapes=[pltpu.VMEM((1, LANE), F32)],
        compiler_params=_cp(("arbitrary",)),
        name="router",
    )(x2, g.reshape(1, d), mod, rw, rb)
    return outs


def _mm_kernel(a_ref, b_ref, o_ref):
    o_ref[...] = _dot(a_ref[...], b_ref[...]).astype(o_ref.dtype)


def _matmul(a, b, out_dtype, tm, tn, name):
    m, k = a.shape
    n = b.shape[1]
    tm = min(tm, m)
    return pl.pallas_call(
        _mm_kernel,
        grid=(m // tm, n // tn),
        in_specs=[pl.BlockSpec((tm, k), lambda i, j: (i, 0)),
                  pl.BlockSpec((k, tn), lambda i, j: (0, j))],
        out_specs=pl.BlockSpec((tm, tn), lambda i, j: (i, j)),
        out_shape=jax.ShapeDtypeStruct((m, n), out_dtype),
        compiler_params=_cp(("parallel", "parallel")),
        name=name,
    )(a, b)


def _out_proj_kernel(y_ref, w_ref, x_ref, mod_ref, o_ref):
    g = mod_ref[0][2:3]
    o_ref[...] = x_ref[...] + g * _dot(y_ref[...], w_ref[...])


def _out_proj(y, w, x2, mod, seq):
    m, k = y.shape
    n = w.shape[1]
    tm = min(1024, seq)
    tn = 512
    per_b = seq // tm
    return pl.pallas_call(
        _out_proj_kernel,
        grid=(m // tm, n // tn),
        in_specs=[pl.BlockSpec((tm, k), lambda i, j: (i, 0)),
                  pl.BlockSpec((k, tn), lambda i, j: (0, j)),
                  pl.BlockSpec((tm, tn), lambda i, j: (i, j)),
                  pl.BlockSpec((1, 6, tn), lambda i, j: (i // per_b, 0, j))],
        out_specs=pl.BlockSpec((tm, tn), lambda i, j: (i, j)),
        out_shape=jax.ShapeDtypeStruct((m, n), F32),
        compiler_params=_cp(("parallel", "parallel")),
        name="out_proj",
    )(y, w, x2, mod)


def _prep_kernel(p_ref, g_ref, q_ref, w_ref):
    tm = p_ref.shape[0]
    lane = lax.broadcasted_iota(I32, (tm, LANE), 1)
    low = lane < (LANE // 2)

    def blk(c):
        return p_ref[:, c * LANE:(c + 1) * LANE]

    def put(c, v):
        q_ref[:, c * LANE:(c + 1) * LANE] = v.astype(BF16)

    def norm_full(c, row):
        y = blk(c)
        ms = jnp.mean(y * y, axis=-1, keepdims=True)
        put(c, y * lax.rsqrt(ms + RMS_EPS) * g_ref[row:row + 1, :])

    def norm_half(c, row):
        y = blk(c)
        yy = y * y
        s_lo = jnp.sum(jnp.where(low, yy, 0.0), axis=-1, keepdims=True)
        s_hi = jnp.sum(jnp.where(low, 0.0, yy), axis=-1, keepdims=True)
        ms = jnp.where(low, s_lo, s_hi) * (2.0 / LANE)
        put(c, y * lax.rsqrt(ms + RMS_EPS) * g_ref[row:row + 1, :])

    for c in range(QKV_PAD // LANE):
        if c < 4:
            norm_half(c, 0)
        elif c < 8:
            norm_half(c, 1)
        elif 12 <= c < 16:
            put(c, blk(c) * g_ref[6:7, :])
        elif 24 <= c < 28:
            norm_full(c, 2)
        elif 28 <= c < 32:
            norm_full(c, 3)
        elif 36 <= c < 40:
            norm_full(c, 4)
        elif c == DK_BLOCK:
            norm_full(c, 5)
        elif c == IK_BLOCK:
            y = blk(c)
            w_ref[...] = y
            put(c, jnp.where(low, y, pltpu.roll(y, LANE // 2, 1)))
        else:
            put(c, blk(c))


def _prep(p, gains):
    t = p.shape[0]
    tm = 256
    return pl.pallas_call(
        _prep_kernel,
        grid=(t // tm,),
        in_specs=[pl.BlockSpec((tm, QKV_PAD), lambda i: (i, 0)),
                  pl.BlockSpec((8, LANE), lambda i: (0, 0))],
        out_specs=[pl.BlockSpec((tm, QKV_PAD), lambda i: (i, 0)),
                   pl.BlockSpec((tm, LANE), lambda i: (i, 0))],
        out_shape=[jax.ShapeDtypeStruct((t, QKV_PAD), BF16),
                   jax.ShapeDtypeStruct((t, LANE), F32)],
        compiler_params=_cp(("parallel",)),
        name="qk_prep",
    )(p, gains)


ATT_TILE = 256


def _dot_tn(a, b):
    return lax.dot_general(a, b, (((0,), (0,)), ((), ())), preferred_element_type=F32)


def _flash_init(s, v, m_ref, l_ref, acc_ref, h):
    m = jnp.max(s, axis=0, keepdims=True)
    p = jnp.exp(s - m)
    m_ref[h] = m
    l_ref[h] = jnp.sum(p, axis=0, keepdims=True)
    acc_ref[h] = _dot_tn(v, p.astype(BF16))


def _flash_step(s, v, m_ref, l_ref, acc_ref, h, keep=None):
    m = m_ref[h]
    m_new = jnp.maximum(m, jnp.max(s, axis=0, keepdims=True))
    alpha = jnp.exp(m - m_new)
    p = jnp.exp(s - m_new)
    if keep is not None:
        p = jnp.where(keep, p, 0.0)
    l_ref[h] = alpha * l_ref[h] + jnp.sum(p, axis=0, keepdims=True)
    acc_ref[h] = alpha * acc_ref[h] + _dot_tn(v, p.astype(BF16))
    m_ref[h] = m_new


def _head_cols(h):
    return slice(h * LANE, (h + 1) * LANE)


def _diff_kernel(sc_ref, q_ref, k_ref, v_ref, g_ref, o_ref, q12_ref, m_ref, l_ref, acc_ref, *, t, slopes):
    i = pl.program_id(1)
    lam = sc_ref[0]
    post = sc_ref[1]
    lane = lax.broadcasted_iota(I32, (t, LANE), 1)
    low = lane < (LANE // 2)
    causal = lax.broadcasted_iota(I32, (t, t), 0) <= lax.broadcasted_iota(I32, (t, t), 1)
    kpos = lax.broadcasted_iota(I32, (t, 1), 0).astype(F32)
    off_d = pl.multiple_of(i * t, t)
    for h in range(A_HEADS):
        cs = _head_cols(h)
        q = q_ref[:, cs]
        zero = jnp.zeros_like(q)
        q12_ref[h] = jnp.concatenate([jnp.where(low, q, zero), jnp.where(low, zero, q)], axis=0)
        bias = jnp.where(causal, slopes[h] * kpos, NEG_BIG)
        s = _dot_t(k_ref[pl.ds(off_d, t), cs], q12_ref[h]) + jnp.concatenate([bias, bias], axis=1)
        _flash_init(s, v_ref[pl.ds(off_d, t), cs], m_ref, l_ref, acc_ref, h)

    def body(kt, _):
        off = pl.multiple_of(kt * t, t)
        shift = ((kt - i) * t).astype(F32)
        for h in range(A_HEADS):
            cs = _head_cols(h)
            s = _dot_t(k_ref[pl.ds(off, t), cs], q12_ref[h]) + slopes[h] * (kpos + shift)
            _flash_step(s, v_ref[pl.ds(off, t), cs], m_ref, l_ref, acc_ref, h)
        return 0

    lax.fori_loop(0, i, body, 0)
    for h in range(A_HEADS):
        o = acc_ref[h] / l_ref[h]
        o = o[:, :t] - lam * o[:, t:]
        ms = jnp.mean(o * o, axis=0, keepdims=True)
        o = o * lax.rsqrt(ms + RMS_EPS) * g_ref[...] * post
        o_ref[:, _head_cols(h)] = o.T.astype(BF16)


def _diff_attention(qkv, scal, subln_g, bsz, seq, slopes):
    t = ATT_TILE
    nq = seq // t
    hw = A_HEADS * LANE
    return pl.pallas_call(
        functools.partial(_diff_kernel, t=t, slopes=slopes),
        grid=(bsz, nq),
        in_specs=[pl.BlockSpec(memory_space=pltpu.SMEM),
                  pl.BlockSpec((t, hw), lambda b, i: (b * nq + i, 0)),
                  pl.BlockSpec((seq, hw), lambda b, i: (b, 1)),
                  pl.BlockSpec((seq, hw), lambda b, i: (b, 2)),
                  pl.BlockSpec((LANE, 1), lambda b, i: (0, 0))],
        out_specs=pl.BlockSpec((t, hw), lambda b, i: (b * nq + i, 0)),
        out_shape=jax.ShapeDtypeStruct((bsz * seq, hw), BF16),
        scratch_shapes=[pltpu.VMEM((A_HEADS, 2 * t, LANE), BF16),
                        pltpu.VMEM((A_HEADS, 1, 2 * t), F32),
                        pltpu.VMEM((A_HEADS, 1, 2 * t), F32),
                        pltpu.VMEM((A_HEADS, LANE, 2 * t), F32)],
        compiler_params=_cp(("parallel", "parallel")),
        name="diff_attn",
    )(scal, qkv, qkv, qkv, subln_g.reshape(LANE, 1))


def _stick_kernel(q_ref, k_ref, v_ref, o_ref, tail_ref, acc_ref, *, t):
    i = pl.program_id(1)
    r = lax.broadcasted_iota(I32, (t, t), 0)
    c = lax.broadcasted_iota(I32, (t, t), 1)
    strict = r < c
    later = jnp.where(c > r, 1.0, 0.0).astype(BF16)

    def tile(off, h, masked):
        cs = _head_cols(h)
        z = _dot_t(k_ref[pl.ds(off, t), cs], q_ref[:, cs])
        ls = jnp.minimum(z, 0.0) - jnp.log1p(jnp.exp(-jnp.abs(z)))
        l1 = ls - z
        if masked:
            l1 = jnp.where(strict, l1, 0.0)
        hi, lo = _split(l1)
        after = _dot(later, hi) + _dot(later, lo) + tail_ref[h]
        w = jnp.exp(ls + after)
        if masked:
            w = jnp.where(strict, w, 0.0)
        acc_ref[h] = acc_ref[h] + _dot_tn(v_ref[pl.ds(off, t), cs], w.astype(BF16))
        tail_ref[h] = tail_ref[h] + jnp.sum(l1, axis=0, keepdims=True)

    tail_ref[...] = jnp.zeros_like(tail_ref)
    acc_ref[...] = jnp.zeros_like(acc_ref)
    off_d = pl.multiple_of(i * t, t)
    for h in range(B_HEADS):
        tile(off_d, h, True)

    def body(j, _):
        off = pl.multiple_of((i - 1 - j) * t, t)
        for h in range(B_HEADS):
            tile(off, h, False)
        return 0

    lax.fori_loop(0, i, body, 0)
    for h in range(B_HEADS):
        o_ref[:, _head_cols(h)] = acc_ref[h].T.astype(BF16)


def _stick_attention(qkv, bsz, seq):
    t = ATT_TILE
    nq = seq // t
    hw = B_HEADS * LANE
    return pl.pallas_call(
        functools.partial(_stick_kernel, t=t),
        grid=(bsz, nq),
        in_specs=[pl.BlockSpec((t, hw), lambda b, i: (b * nq + i, 3)),
                  pl.BlockSpec((seq, hw), lambda b, i: (b, 4)),
                  pl.BlockSpec((seq, hw), lambda b, i: (b, 5))],
        out_specs=pl.BlockSpec((t, hw), lambda b, i: (b * nq + i, 0)),
        out_shape=jax.ShapeDtypeStruct((bsz * seq, hw), BF16),
        scratch_shapes=[pltpu.VMEM((B_HEADS, 1, t), F32),
                        pltpu.VMEM((B_HEADS, LANE, t), F32)],
        compiler_params=_cp(("parallel", "parallel")),
        name="stick_attn",
    )(qkv, qkv, qkv)


def _moba_kernel(q_ref, k_ref, v_ref, o_ref, sel_ref, m_ref, l_ref, acc_ref, *, t, n_kb, n_sel, slopes):
    i = pl.program_id(1)
    seq = k_ref.shape[0]
    blk = lax.broadcasted_iota(I32, (LANE, t), 0)
    causal = lax.broadcasted_iota(I32, (t, t), 0) <= lax.broadcasted_iota(I32, (t, t), 1)
    kpos = lax.broadcasted_iota(I32, (t, 1), 0).astype(F32)
    lo_edge = lax.broadcasted_iota(I32, (LANE, seq), 0) * t
    bc = lax.broadcasted_iota(I32, (LANE, seq), 1)
    avg = jnp.where((bc >= lo_edge) & (bc < lo_edge + t), 1.0 / t, 0.0).astype(BF16)
    off_d = pl.multiple_of(i * t, t)
    for h in range(C_HEADS):
        cs = _head_cols(h)
        q = q_ref[:, cs]
        k_mean = _dot(avg, k_ref[:, cs]).astype(BF16)
        gate = jnp.where(blk < i, _dot_t(k_mean, q), -jnp.inf)
        rank = jnp.zeros((LANE, t), F32)
        for mth in range(n_kb):
            gm = gate[mth:mth + 1, :]
            beats = (gm > gate) | ((gm == gate) & (blk > mth))
            rank = rank + jnp.where(beats, 1.0, 0.0)
        sel_ref[h] = jnp.where((rank < n_sel) & (blk < i), 1.0, 0.0)
        bias = jnp.where(causal, slopes[h] * kpos, NEG_BIG)
        s = _dot_t(k_ref[pl.ds(off_d, t), cs], q) + bias
        _flash_init(s, v_ref[pl.ds(off_d, t), cs], m_ref, l_ref, acc_ref, h)

    def body(kt, _):
        off = pl.multiple_of(kt * t, t)
        shift = ((kt - i) * t).astype(F32)
        for h in range(C_HEADS):
            cs = _head_cols(h)
            keep = jnp.broadcast_to(sel_ref[h, pl.ds(kt, 1), :], (t, t)) > 0.0
            s = _dot_t(k_ref[pl.ds(off, t), cs], q_ref[:, cs]) + slopes[h] * (kpos + shift)
            s = jnp.where(keep, s, NEG_BIG)
            _flash_step(s, v_ref[pl.ds(off, t), cs], m_ref, l_ref, acc_ref, h, keep=keep)
        return 0

    lax.fori_loop(0, i, body, 0)
    for h in range(C_HEADS):
        o_ref[:, _head_cols(h)] = (acc_ref[h] / l_ref[h]).T.astype(BF16)


def _moba_attention(qkv, bsz, seq, slopes):
    t = MOBA_BLOCK
    assert seq % t == 0 and t == ATT_TILE
    n_kb = seq // t
    assert n_kb <= LANE
    n_sel = min(MOBA_TOPK, n_kb - 1)
    hw = C_HEADS * LANE
    return pl.pallas_call(
        functools.partial(_moba_kernel, t=t, n_kb=n_kb, n_sel=n_sel, slopes=slopes),
        grid=(bsz, n_kb),
        in_specs=[pl.BlockSpec((t, hw), lambda b, i: (b * n_kb + i, 6)),
                  pl.BlockSpec((seq, hw), lambda b, i: (b, 7)),
                  pl.BlockSpec((seq, hw), lambda b, i: (b, 8))],
        out_specs=pl.BlockSpec((t, hw), lambda b, i: (b * n_kb + i, 0)),
        out_shape=jax.ShapeDtypeStruct((bsz * seq, hw), BF16),
        scratch_shapes=[pltpu.VMEM((C_HEADS, LANE, t), F32),
                        pltpu.VMEM((C_HEADS, 1, t), F32),
                        pltpu.VMEM((C_HEADS, 1, t), F32),
                        pltpu.VMEM((C_HEADS, LANE, t), F32)],
        compiler_params=_cp(("parallel", "parallel")),
        name="moba_attn",
    )(qkv, qkv, qkv)


def _dsa_kernel(q_ref, k_ref, v_ref, iq_ref, ki_ref, w_ref, o_ref,
                keys_ref, iqm_ref, cut_ref, m_ref, l_ref, acc_ref, *, t, n_keep, idx_bits, slopes):
    i = pl.program_id(1)
    n_kt = i + 1
    lane = lax.broadcasted_iota(I32, (t, LANE), 1)
    low = lane < (LANE // 2)
    kidx = lax.broadcasted_iota(I32, (t, t), 0)
    qpos = i * t + lax.broadcasted_iota(I32, (t, t), 1)
    kpos = lax.broadcasted_iota(I32, (t, 1), 0).astype(F32)
    w_scale = (IDX_DH ** -0.5) * (IDX_HEADS ** -0.5)
    w_t = w_ref[...].T * w_scale
    for p in range(IDX_HEADS // 2):
        qp = iq_ref[:, p * LANE:(p + 1) * LANE]
        zero = jnp.zeros_like(qp)
        iqm_ref[2 * p] = jnp.where(low, qp, zero)
        iqm_ref[2 * p + 1] = jnp.where(low, zero, qp)

    def score_body(kt, _):
        off = pl.multiple_of(kt * t, t)
        kk = ki_ref[pl.ds(off, t), :]
        sc = jnp.zeros((t, t), F32)
        for j in range(IDX_HEADS):
            wj = w_t[LANE // 2 + j:LANE // 2 + j + 1, :]
            sc = sc + jnp.maximum(_dot_t(kk, iqm_ref[j]), 0.0) * wj
        sc = jnp.where(sc == 0.0, 0.0, sc)
        bits = lax.bitcast_convert_type(sc, I32)
        key = jnp.bitwise_xor(bits, jnp.bitwise_and(lax.shift_right_arithmetic(bits, 31), 0x7FFFFFFF))
        keys_ref[pl.ds(off, t), :] = jnp.where(off + kidx <= qpos, key, INT_MIN)
        return 0

    lax.fori_loop(0, n_kt, score_body, 0)

    def count(pred):
        def body(kt, acc):
            off = pl.multiple_of(kt * t, t)
            hit = jnp.where(pred(keys_ref[pl.ds(off, t), :], off + kidx), 1.0, 0.0)
            return acc + jnp.sum(hit, axis=0, keepdims=True)
        return lax.fori_loop(0, n_kt, body, jnp.zeros((1, t), F32))

    keep_f = float(n_keep)
    zero_t = jnp.zeros((1, t), I32)
    c0 = count(lambda kc, col: kc >= zero_t)
    thr = jnp.where(c0 >= keep_f, 0, INT_MIN).astype(I32)

    def bit_body(b, thr):
        cand = jnp.bitwise_or(thr, lax.shift_left(jnp.int32(1), 30 - b))
        cnt = count(lambda kc, col: kc >= cand)
        return jnp.where(cnt >= keep_f, cand, thr)

    thr = lax.fori_loop(0, 31, bit_body, thr)

    n_ge = count(lambda kc, col: kc >= thr)
    n_gt = count(lambda kc, col: kc > thr)
    need = (n_ge > keep_f) & (thr > INT_MIN)
    quota = keep_f - n_gt
    big = jnp.int32(2 ** 30)
    cut_ref[...] = jnp.full((1, t), big, I32)

    @pl.when(jnp.max(jnp.where(need, 1.0, 0.0)) > 0.0)
    def _():
        def tie_body(b, cut):
            cand = jnp.bitwise_or(cut, lax.shift_left(jnp.int32(1), idx_bits - 1 - b))
            cnt = count(lambda kc, col: (kc == thr) & (col < cand))
            return jnp.where(cnt <= quota, cand, cut)
        cut = lax.fori_loop(0, idx_bits, tie_body, jnp.zeros((1, t), I32))
        cut_ref[...] = jnp.where(need, cut, big)

    cut = cut_ref[...]

    def flag_body(kt, _):
        off = pl.multiple_of(kt * t, t)
        kc = keys_ref[pl.ds(off, t), :]
        col = off + kidx
        sel = ((kc > thr) | ((kc == thr) & (col < cut))) & (col <= qpos)
        keys_ref[pl.ds(off, t), :] = jnp.where(sel, 1, 0).astype(I32)
        return 0

    lax.fori_loop(0, n_kt, flag_body, 0)

    m_ref[...] = jnp.full(m_ref.shape, NEG_BIG, F32)
    l_ref[...] = jnp.zeros_like(l_ref)
    acc_ref[...] = jnp.zeros_like(acc_ref)

    def body(kt, _):
        off = pl.multiple_of(kt * t, t)
        shift = ((kt - i) * t).astype(F32)
        keep = keys_ref[pl.ds(off, t), :] > 0
        k = k_ref[pl.ds(off, t), :]
        v = v_ref[pl.ds(off, t), :]
        for h in range(D_HEADS):
            s = _dot_t(k, q_ref[:, _head_cols(h)]) + slopes[h] * (kpos + shift)
            s = jnp.where(keep, s, NEG_BIG)
            _flash_step(s, v, m_ref, l_ref, acc_ref, h, keep=keep)
        return 0

    lax.fori_loop(0, n_kt, body, 0)
    for h in range(D_HEADS):
        o_ref[:, _head_cols(h)] = (acc_ref[h] / l_ref[h]).T.astype(BF16)


def _dsa_attention(qkv, wraw, bsz, seq, slopes):
    t = ATT_TILE
    nq = seq // t
    n_keep = min(DSA_TOPK, seq // 4)
    idx_bits = int(math.ceil(math.log2(seq))) + 1
    hw = D_HEADS * LANE
    iw = IDX_HEADS * IDX_DH
    return pl.pallas_call(
        functools.partial(_dsa_kernel, t=t, n_keep=n_keep, idx_bits=idx_bits, slopes=slopes),
        grid=(bsz, nq),
        in_specs=[pl.BlockSpec((t, hw), lambda b, i: (b * nq + i, 9)),
                  pl.BlockSpec((seq, LANE), lambda b, i: (b, DK_BLOCK)),
                  pl.BlockSpec((seq, LANE), lambda b, i: (b, DV_BLOCK)),
                  pl.BlockSpec((t, iw), lambda b, i: (b * nq + i, IQ_BLOCK * LANE // iw)),
                  pl.BlockSpec((seq, LANE), lambda b, i: (b, IK_BLOCK)),
                  pl.BlockSpec((t, LANE), lambda b, i: (b * nq + i, 0))],
        out_specs=pl.BlockSpec((t, hw), lambda b, i: (b * nq + i, 0)),
        out_shape=jax.ShapeDtypeStruct((bsz * seq, hw), BF16),
        scratch_shapes=[pltpu.VMEM((seq, t), I32),
                        pltpu.VMEM((IDX_HEADS, t, LANE), BF16),
                        pltpu.VMEM((1, t), I32),
                        pltpu.VMEM((D_HEADS, 1, t), F32),
                        pltpu.VMEM((D_HEADS, 1, t), F32),
                        pltpu.VMEM((D_HEADS, LANE, t), F32)],
        compiler_params=_cp(("parallel", "parallel")),
        name="dsa_attn",
    )(qkv, qkv, qkv, qkv, qkv, wraw)


def _merge_kernel(a_ref, b_ref, c_ref, d_ref, wb_ref, g0, g1, g2, g3, gb_ref, y_ref):
    acc = None
    for n, (br, gl) in enumerate(zip((a_ref, b_ref, c_ref, d_ref), (g0, g1, g2, g3))):
        up = _dot(br[...], wb_ref[n])
        gate = jax.nn.sigmoid(gl[...].astype(F32) + gb_ref[n])
        acc = gate * up if acc is None else acc + gate * up
    y_ref[...] = acc.astype(BF16)


def _merge(branches, wb, gl, gate_b):
    t, bw = branches[0].shape
    d = wb.shape[2]
    tm = min(1024, t)
    tn = 512
    nj = d // tn
    br_spec = pl.BlockSpec((tm, bw), lambda j, i: (i, 0))
    gl_specs = [pl.BlockSpec((tm, tn), functools.partial(lambda j, i, n: (i, n * nj + j), n=n))
                for n in range(N_BRANCH)]
    return pl.pallas_call(
        _merge_kernel,
        grid=(nj, t // tm),
        in_specs=[br_spec] * 4 + [pl.BlockSpec((N_BRANCH, bw, tn), lambda j, i: (0, 0, j))] + gl_specs
                 + [pl.BlockSpec((N_BRANCH, 1, tn), lambda j, i: (0, 0, j))],
        out_specs=pl.BlockSpec((tm, tn), lambda j, i: (i, j)),
        out_shape=jax.ShapeDtypeStruct((t, d), BF16),
        compiler_params=_cp(("parallel", "parallel")),
        name="merge",
    )(*branches, wb, gl, gl, gl, gl, gate_b.reshape(N_BRANCH, 1, d))


def _gather_rows_kernel(tok_ref, src_ref, out_ref, sem, *, rows):
    base = pl.program_id(0) * rows

    def issue(r, _):
        pltpu.make_async_copy(src_ref.at[pl.ds(tok_ref[base + r], 1)], out_ref.at[pl.ds(r, 1)], sem).start()
        return 0

    lax.fori_loop(0, rows, issue, 0, unroll=8)
    pltpu.make_async_copy(src_ref.at[pl.ds(0, rows)], out_ref, sem).wait()


def _gather_rows(src, row_tok):
    n_rows = row_tok.shape[0]
    rows = max(r for r in (4 * MOE_ROWS, 2 * MOE_ROWS, MOE_ROWS) if n_rows % r == 0)
    return pl.pallas_call(
        functools.partial(_gather_rows_kernel, rows=rows),
        grid_spec=pltpu.PrefetchScalarGridSpec(
            num_scalar_prefetch=1,
            grid=(n_rows // rows,),
            in_specs=[pl.BlockSpec(memory_space=pl.ANY)],
            out_specs=pl.BlockSpec((rows, src.shape[1]), lambda i, tok: (i, 0)),
            scratch_shapes=[pltpu.SemaphoreType.DMA]),
        out_shape=jax.ShapeDtypeStruct((n_rows, src.shape[1]), src.dtype),
        compiler_params=_cp(("arbitrary",)),
        name="moe_gather",
    )(row_tok, src)


def _unpack_rows(xp):
    lo = lax.bitcast_convert_type(lax.shift_left(xp, 16), F32).astype(BF16)
    hi = lax.bitcast_convert_type(jnp.bitwise_and(xp, -65536), F32).astype(BF16)
    return lo, hi


def _expert_changed(be_ref, i):
    return (i == 0) | (be_ref[i] != be_ref[jnp.maximum(i - 1, 0)])


def _expert_up_kernel(be_ref, nu_ref, xp_ref, wg_ref, wl_ref, bg_ref, bl_ref, act_ref, wg_s, wl_s):
    i = pl.program_id(1)
    live = i < nu_ref[0]

    @pl.when(live & _expert_changed(be_ref, i))
    def _():
        wg_s[...] = wg_ref[0, 0].astype(BF16)
        wl_s[...] = wl_ref[0, 0].astype(BF16)

    @pl.when(live)
    def _():
        half = xp_ref.shape[1]
        lo, hi = _unpack_rows(xp_ref[...])
        glu = _dot(lo, wg_s[:half, :]) + _dot(hi, wg_s[half:, :]) + bg_ref[0]
        lin = _dot(lo, wl_s[:half, :]) + _dot(hi, wl_s[half:, :]) + bl_ref[0]
        glu = jnp.minimum(glu, SWIGLU_LIMIT)
        lin = jnp.clip(lin, -SWIGLU_LIMIT, SWIGLU_LIMIT)
        act = glu * jax.nn.sigmoid(SWIGLU_ALPHA * glu) * (lin + 1.0)
        act_ref[...] = act.astype(BF16)

    @pl.when(jnp.logical_not(live))
    def _():
        act_ref[...] = jnp.zeros_like(act_ref)


def _expert_up(block_exp, n_used, xs, w1, b1, d_ff, l):
    n_rows, half = xs.shape
    d = 2 * half
    mb = MOE_ROWS
    tf = 512
    nf = d_ff // tf
    n_blocks = n_rows // mb

    def rows(j, i, be, nu):
        return (jnp.minimum(i, nu[0] - 1), 0)

    return pl.pallas_call(
        _expert_up_kernel,
        grid_spec=pltpu.PrefetchScalarGridSpec(
            num_scalar_prefetch=2,
            grid=(nf, n_blocks),
            in_specs=[pl.BlockSpec((mb, half), rows),
                      pl.BlockSpec((1, 1, d, tf), lambda j, i, be, nu: (l, be[i], 0, j)),
                      pl.BlockSpec((1, 1, d, tf), lambda j, i, be, nu: (l, be[i], 0, nf + j)),
                      pl.BlockSpec((1, 1, tf), lambda j, i, be, nu: (be[i], 0, j)),
                      pl.BlockSpec((1, 1, tf), lambda j, i, be, nu: (be[i], 0, nf + j))],
            out_specs=pl.BlockSpec((mb, tf), lambda j, i, be, nu: (i, j)),
            scratch_shapes=[pltpu.VMEM((d, tf), BF16), pltpu.VMEM((d, tf), BF16)]),
        out_shape=jax.ShapeDtypeStruct((n_rows, d_ff), BF16),
        compiler_params=_cp(("arbitrary", "arbitrary")),
        name="expert_up",
    )(block_exp, n_used, xs, w1, w1, b1, b1)


def _expert_down_kernel(be_ref, nu_ref, a_ref, w_ref, b_ref, y_ref, w_s):
    i = pl.program_id(1)
    live = i < nu_ref[0]

    @pl.when(live & _expert_changed(be_ref, i))
    def _():
        w_s[...] = w_ref[0, 0].astype(BF16)

    @pl.when(live)
    def _():
        y_ref[...] = _dot(a_ref[...], w_s[...]) + b_ref[0]

    @pl.when(jnp.logical_not(live))
    def _():
        y_ref[...] = jnp.zeros_like(y_ref)


def _expert_down(block_exp, n_used, act, w2, b2, l):
    n_rows, d_ff = act.shape
    d = w2.shape[3]
    mb = MOE_ROWS
    tn = 1024
    nn = d // tn
    n_blocks = n_rows // mb
    return pl.pallas_call(
        _expert_down_kernel,
        grid_spec=pltpu.PrefetchScalarGridSpec(
            num_scalar_prefetch=2,
            grid=(nn, n_blocks),
            in_specs=[pl.BlockSpec((mb, d_ff), lambda j, i, be, nu: (jnp.minimum(i, nu[0] - 1), 0)),
                      pl.BlockSpec((1, 1, d_ff, tn), lambda j, i, be, nu: (l, be[i], 0, j)),
                      pl.BlockSpec((1, 1, tn), lambda j, i, be, nu: (be[i], 0, j))],
            out_specs=pl.BlockSpec((mb, tn), lambda j, i, be, nu: (i, j)),
            scratch_shapes=[pltpu.VMEM((d_ff, tn), BF16)]),
        out_shape=jax.ShapeDtypeStruct((n_rows, d), F32),
        compiler_params=_cp(("arbitrary", "arbitrary")),
        name="expert_down",
    )(block_exp, n_used, act, w2, b2)


def _combine_kernel(dest_ref, y_ref, x_ref, gate_ref, mod_ref, o_ref, buf, sem, *, tm):
    base = pl.program_id(0) * tm * TOP_K

    def issue(r, _):
        for k in range(TOP_K):
            pltpu.make_async_copy(y_ref.at[pl.ds(dest_ref[base + r * TOP_K + k], 1)],
                                  buf.at[pl.ds(k * tm + r, 1)], sem).start()
        return 0

    lax.fori_loop(0, tm, issue, 0, unroll=4)
    pltpu.make_async_copy(y_ref.at[pl.ds(0, TOP_K * tm)], buf, sem).wait()
    gate = gate_ref[...]
    acc = gate[:, 0:1] * buf[0:tm, :]
    for k in range(1, TOP_K):
        acc = acc + gate[:, k:k + 1] * buf[k * tm:(k + 1) * tm, :]
    o_ref[...] = x_ref[...] + mod_ref[0][5:6] * acc


def _combine(dest, y, x2, gate, mod, seq):
    t, d = x2.shape
    tm = min(256, seq)
    per_b = seq // tm
    return pl.pallas_call(
        functools.partial(_combine_kernel, tm=tm),
        grid_spec=pltpu.PrefetchScalarGridSpec(
            num_scalar_prefetch=1,
            grid=(t // tm,),
            in_specs=[pl.BlockSpec(memory_space=pl.ANY),
                      pl.BlockSpec((tm, d), lambda i, dst: (i, 0)),
                      pl.BlockSpec((tm, LANE), lambda i, dst: (i, 0)),
                      pl.BlockSpec((1, 6, d), lambda i, dst: (i // per_b, 0, 0))],
            out_specs=pl.BlockSpec((tm, d), lambda i, dst: (i, 0)),
            scratch_shapes=[pltpu.VMEM((TOP_K * tm, d), F32), pltpu.SemaphoreType.DMA]),
        out_shape=jax.ShapeDtypeStruct((t, d), F32),
        compiler_params=_cp(("arbitrary",)),
        name="moe_combine",
    )(dest, y, x2, gate, mod)


def _moe(x2, g, mod, rw, rb, w1, b1, w2, b2, seq, l):
    t, d = x2.shape
    _, n_exp, _, two_f = w1.shape
    d_ff = two_f // 2
    rw_p = jnp.pad(rw, ((0, 0), (0, LANE - n_exp))).astype(BF16)
    rb_p = jnp.pad(rb, (0, LANE - n_exp)).reshape(1, LANE)
    hp, idx, gate, rank, cnt = _router(x2, g, mod, rw_p, rb_p, seq, n_exp)

    mb = MOE_ROWS
    counts = cnt[0, :n_exp]
    padded = (counts + mb - 1) // mb * mb
    pend = jnp.cumsum(padded)
    pstart = pend - padded
    top_idx = idx[:, :TOP_K]
    dest = (pstart[top_idx] + rank[:, :TOP_K]).reshape(-1).astype(I32)
    n_assign = t * TOP_K
    n_blocks = -(-(n_assign + n_exp * (mb - 1)) // mb)
    n_rows = n_blocks * mb
    tok = (jnp.arange(n_assign, dtype=I32) // TOP_K)
    row_tok = jnp.zeros((n_rows,), I32).at[dest].set(tok)
    block_start = jnp.arange(n_blocks, dtype=I32) * mb
    block_exp = jnp.minimum(jnp.sum(pend[None, :] <= block_start[:, None], axis=1), n_exp - 1).astype(I32)
    n_used = (pend[-1:] // mb).astype(I32)

    xs = _gather_rows(hp, row_tok)
    act = _expert_up(block_exp, n_used, xs, w1, b1.reshape(n_exp, 1, two_f), d_ff, l)
    y = _expert_down(block_exp, n_used, act, w2, b2.reshape(n_exp, 1, d), l)
    return _combine(dest, y, x2, gate, mod, seq)


def _token_mixer(x2, l, mod, bsz, seq, norm1_g, w_in, gate_b, a_qn_g, a_kn_g, a_lam_q1, a_lam_k1,
                 a_lam_q2, a_lam_k2, a_subln_g, c_qn_g, c_kn_g, d_qn_g, d_kn_g, w_branch, w_out):
    d = x2.shape[1]
    slopes = _alibi_slopes()
    h = _norm1(x2, norm1_g, mod, seq)
    dq_end = DK_BLOCK * LANE - IDX_HEADS * IDX_DH
    iq_lo = dq_end + 2 * D_DH
    iq_hi = iq_lo + IDX_HEADS * IDX_DH
    w_qkv = jnp.concatenate([w_in[:, :dq_end], w_in[:, iq_lo:iq_hi], w_in[:, dq_end:iq_lo],
                             w_in[:, iq_hi:QKV_COLS], jnp.zeros((d, QKV_PAD - QKV_COLS), F32)],
                            axis=1).astype(BF16)
    w_gl = w_in[:, QKV_COLS:].astype(BF16)
    p = _matmul(h, w_qkv, F32, 1024, 512, "in_proj_qkv")
    gl = _matmul(h, w_gl, BF16, 1024, 512, "in_proj_gate")

    gains = jnp.stack([
        jnp.tile(a_qn_g, 2) * (A_DH ** -0.5),
        jnp.tile(a_kn_g, 2),
        c_qn_g * (C_DH ** -0.5),
        c_kn_g,
        d_qn_g * (D_DH ** -0.5),
        d_kn_g,
        jnp.full((LANE,), B_DH ** -0.5, F32),
        jnp.ones((LANE,), F32)]).astype(F32)
    qkv, wraw = _prep(p, gains)

    lambda_init = 0.8 - 0.6 * math.exp(-0.3 * l)
    lam = (jnp.exp(jnp.sum(a_lam_q1 * a_lam_k1)) - jnp.exp(jnp.sum(a_lam_q2 * a_lam_k2)) + lambda_init)
    scal = jnp.stack([lam, jnp.float32(1.0 - lambda_init)]).astype(F32)

    oa = _diff_attention(qkv, scal, a_subln_g, bsz, seq, slopes[0::3])
    ob = _stick_attention(qkv, bsz, seq)
    oc = _moba_attention(qkv, bsz, seq, slopes[1::3])
    od = _dsa_attention(qkv, wraw, bsz, seq, slopes[2::3])

    y = _merge([oa, ob, oc, od], w_branch.astype(BF16), gl, gate_b)
    return _out_proj(y, w_out.astype(BF16), x2, mod, seq)


def kernel(x, c, ada_w, ada_b, norm1_g, norm2_g, w_in, gate_b, a_qn_g, a_kn_g, a_lam_q1, a_lam_k1,
           a_lam_q2, a_lam_k2, a_subln_g, c_qn_g, c_kn_g, d_qn_g, d_kn_g, w_branch, w_out,
           router_w, router_b, w1, b1, w2, b2):
    bsz, seq, d = x.shape
    depth = ada_w.shape[0]
    mods = _ada_mod(c, ada_w, ada_b)
    x2 = x.reshape(bsz * seq, d)
    for l in range(depth):
        mod = mods[l]
        x2 = _token_mixer(x2, l, mod, bsz, seq, norm1_g[l], w_in[l], gate_b[l], a_qn_g[l], a_kn_g[l],
                          a_lam_q1[l], a_lam_k1[l], a_lam_q2[l], a_lam_k2[l], a_subln_g[l],
                          c_qn_g[l], c_kn_g[l], d_qn_g[l], d_kn_g[l], w_branch[l], w_out[l])
        x2 = _moe(x2, norm2_g[l], mod, router_w[l], router_b[l], w1, b1[l], w2, b2[l], seq, l)
    return x2.reshape(bsz, seq, d)
```

```python
import functools
import math

import jax
import jax.numpy as jnp
from jax import lax
from jax.experimental import pallas as pl
from jax.experimental.pallas import tpu as pltpu

F32 = jnp.float32
BF16 = jnp.bfloat16
I32 = jnp.int32

A_HEADS, A_DH = 4, 64
B_HEADS, B_DH = 4, 128
C_HEADS, C_DH = 4, 128
D_HEADS, D_DH = 4, 128
IDX_HEADS, IDX_DH = 8, 64
N_BRANCH = 4
MOBA_BLOCK = 256
MOBA_TOPK = 3
DSA_TOPK = 256
TOP_K = 4
SWIGLU_ALPHA = 1.702
SWIGLU_LIMIT = 7.0
N_ALIBI = A_HEADS + C_HEADS + D_HEADS
RMS_EPS = 1e-6

LANE = 128
QKV_COLS = 5960
QKV_PAD = 6144
IQ_BLOCK = 40
DK_BLOCK = 44
DV_BLOCK = 45
IK_BLOCK = 46
NEG_BIG = -1e30
INT_MIN = -(2 ** 31)
MOE_SUB = 256
MOE_BLK = 512
VMEM_LIMIT = 56 * 1024 * 1024


def _alibi_slopes():
    return [2.0 ** (-8.0 * (i + 1.0) / N_ALIBI) for i in range(N_ALIBI)]


def _cp(sem, vmem=VMEM_LIMIT):
    return pltpu.CompilerParams(dimension_semantics=sem, vmem_limit_bytes=vmem)


def _dot(a, b):
    return jnp.dot(a, b, preferred_element_type=F32)


def _dot_t(a, b):
    return lax.dot_general(a, b, (((1,), (1,)), ((), ())), preferred_element_type=F32)


def _split(x):
    hi = x.astype(BF16)
    lo = (x - hi.astype(F32)).astype(BF16)
    return hi, lo


def _ada_kernel(c_ref, w_ref, b_ref, o_ref):
    c = c_ref[...]
    cond = c * jax.nn.sigmoid(c)
    ch, cl = _split(cond)
    wh, wl = _split(w_ref[0])
    o_ref[0] = _dot(ch, wh) + _dot(ch, wl) + _dot(cl, wh) + b_ref[0]


def _ada_mod(c, ada_w, ada_b):
    depth, d, n = ada_w.shape
    bsz = c.shape[0]
    bp = -(-bsz // 8) * 8
    cp = jnp.pad(c, ((0, bp - bsz), (0, 0)))
    tn = 1024
    out = pl.pallas_call(
        _ada_kernel,
        grid=(depth, n // tn),
        in_specs=[pl.BlockSpec((bp, d), lambda l, j: (0, 0)),
                  pl.BlockSpec((1, d, tn), lambda l, j: (l, 0, j)),
                  pl.BlockSpec((1, 1, tn), lambda l, j: (l, 0, j))],
        out_specs=pl.BlockSpec((1, bp, tn), lambda l, j: (l, 0, j)),
        out_shape=jax.ShapeDtypeStruct((depth, bp, n), F32),
        compiler_params=_cp(("parallel", "parallel")),
        name="ada_mod",
    )(cp, ada_w, ada_b.reshape(depth, 1, n))
    return out[:, :bsz].reshape(depth, bsz, 6, d)


def _norm_mod(x, g, mod, sh_idx, sc_idx):
    xx = x * x
    ms = jnp.mean(xx, axis=-1, keepdims=True)
    y = x * lax.rsqrt(ms + RMS_EPS) * g
    return y * (1.0 + mod[sc_idx:sc_idx + 1]) + mod[sh_idx:sh_idx + 1]


def _norm1_kernel(x_ref, g_ref, mod_ref, h_ref):
    h = _norm_mod(x_ref[...], g_ref[...], mod_ref[0], 0, 1)
    h_ref[...] = h.astype(BF16)


def _norm1(x2, g, mod, seq):
    t, d = x2.shape
    tm = min(512, seq)
    per_b = seq // tm
    return pl.pallas_call(
        _norm1_kernel,
        grid=(t // tm,),
        in_specs=[pl.BlockSpec((tm, d), lambda i: (i, 0)),
                  pl.BlockSpec((1, d), lambda i: (0, 0)),
                  pl.BlockSpec((1, 6, d), lambda i: (i // per_b, 0, 0))],
        out_specs=pl.BlockSpec((tm, d), lambda i: (i, 0)),
        out_shape=jax.ShapeDtypeStruct((t, d), BF16),
        compiler_params=_cp(("parallel",)),
        name="norm1",
    )(x2, g.reshape(1, d), mod)


def _router_kernel(x_ref, g_ref, mod_ref, rw_ref, rb_ref,
                   hp_ref, idx_ref, gate_ref, rank_ref, cnt_ref, carry_ref, *, n_exp):
    i = pl.program_id(0)
    tm = x_ref.shape[0]
    half = x_ref.shape[1] // 2

    @pl.when(i == 0)
    def _():
        carry_ref[...] = jnp.zeros_like(carry_ref)

    h = _norm_mod(x_ref[...], g_ref[...], mod_ref[0], 3, 4)
    hb = h.astype(BF16)
    lo = lax.bitcast_convert_type(hb[:, :half].astype(F32), I32)
    hi = lax.bitcast_convert_type(hb[:, half:].astype(F32), I32)
    hp_ref[...] = jnp.bitwise_or(lax.shift_right_logical(lo, 16), jnp.bitwise_and(hi, -65536))

    logits = _dot(hb, rw_ref[...]) + rb_ref[...]
    lane = lax.broadcasted_iota(I32, (tm, LANE), 1)
    lane_f = lane.astype(F32)
    cur = jnp.where(lane < n_exp, logits, -jnp.inf)
    idx_out = jnp.zeros((tm, LANE), F32)
    val_out = jnp.zeros((tm, LANE), F32)
    onehot = jnp.zeros((tm, LANE), F32)
    vals = []
    idxs = []
    for k in range(TOP_K):
        m = jnp.max(cur, axis=-1, keepdims=True)
        am = jnp.min(jnp.where(cur == m, lane_f, float(LANE)), axis=-1, keepdims=True)
        hit = lane_f == am
        cur = jnp.where(hit, -jnp.inf, cur)
        onehot = onehot + jnp.where(hit, 1.0, 0.0)
        idx_out = jnp.where(lane == k, am, idx_out)
        vals.append(m)
        idxs.append(am)
    den = sum(jnp.exp(v - vals[0]) for v in vals)
    for k in range(TOP_K):
        val_out = jnp.where(lane == k, jnp.exp(vals[k] - vals[0]) / den, val_out)
    idx_ref[...] = idx_out.astype(I32)
    gate_ref[...] = val_out

    r = lax.broadcasted_iota(I32, (tm, tm), 0)
    c = lax.broadcasted_iota(I32, (tm, tm), 1)
    tri = jnp.where(c < r, 1.0, 0.0).astype(BF16)
    prefix = _dot(tri, onehot.astype(BF16)) + carry_ref[...]
    rank_out = jnp.zeros((tm, LANE), F32)
    for k in range(TOP_K):
        rk = jnp.sum(jnp.where(lane_f == idxs[k], prefix, 0.0), axis=-1, keepdims=True)
        rank_out = jnp.where(lane == k, rk, rank_out)
    rank_ref[...] = rank_out.astype(I32)
    tot = carry_ref[...] + jnp.sum(onehot, axis=0, keepdims=True)
    carry_ref[...] = tot
    cnt_ref[...] = jnp.broadcast_to(tot, cnt_ref.shape).astype(I32)


def _router(x2, g, mod, rw, rb, seq, n_exp):
    t, d = x2.shape
    tm = min(256, seq)
    per_b = seq // tm
    outs = pl.pallas_call(
        functools.partial(_router_kernel, n_exp=n_exp),
        grid=(t // tm,),
        in_specs=[pl.BlockSpec((tm, d), lambda i: (i, 0)),
                  pl.BlockSpec((1, d), lambda i: (0, 0)),
                  pl.BlockSpec((1, 6, d), lambda i: (i // per_b, 0, 0)),
                  pl.BlockSpec((d, LANE), lambda i: (0, 0)),
                  pl.BlockSpec((1, LANE), lambda i: (0, 0))],
        out_specs=[pl.BlockSpec((tm, d // 2), lambda i: (i, 0)),
                   pl.BlockSpec((tm, LANE), lambda i: (i, 0)),
                   pl.BlockSpec((tm, LANE), lambda i: (i, 0)),
                   pl.BlockSpec((tm, LANE), lambda i: (i, 0)),
                   pl.BlockSpec((8, LANE), lambda i: (0, 0))],
        out_shape=[jax.ShapeDtypeStruct((t, d // 2), I32),
                   jax.ShapeDtypeStruct((t, LANE), I32),
                   jax.ShapeDtypeStruct((t, LANE), F32),
                   jax.ShapeDtypeStruct((t, LANE), I32),
                   jax.ShapeDtypeStruct((8, LANE), I32)],
        scratch_shapes=[pltpu.VMEM((1, LANE), F32)],
        compiler_params=_cp(("arbitrary",)),
        name="router",
    )(x2, g.reshape(1, d), mod, rw, rb)
    return outs


def _mm_kernel(a_ref, b_ref, o_ref):
    o_ref[...] = _dot(a_ref[...], b_ref[...]).astype(o_ref.dtype)


def _matmul(a, b, out_dtype, tm, tn, name):
    m, k = a.shape
    n = b.shape[1]
    tm = min(tm, m)
    return pl.pallas_call(
        _mm_kernel,
        grid=(m // tm, n // tn),
        in_specs=[pl.BlockSpec((tm, k), lambda i, j: (i, 0)),
                  pl.BlockSpec((k, tn), lambda i, j: (0, j))],
        out_specs=pl.BlockSpec((tm, tn), lambda i, j: (i, j)),
        out_shape=jax.ShapeDtypeStruct((m, n), out_dtype),
        compiler_params=_cp(("parallel", "parallel")),
        name=name,
    )(a, b)


def _out_proj_kernel(y_ref, w_ref, x_ref, mod_ref, o_ref):
    g = mod_ref[0][2:3]
    o_ref[...] = x_ref[...] + g * _dot(y_ref[...], w_ref[...])


def _out_proj(y, w, x2, mod, seq):
    m, k = y.shape
    n = w.shape[1]
    tm = min(1024, seq)
    tn = 512
    per_b = seq // tm
    return pl.pallas_call(
        _out_proj_kernel,
        grid=(m // tm, n // tn),
        in_specs=[pl.BlockSpec((tm, k), lambda i, j: (i, 0)),
                  pl.BlockSpec((k, tn), lambda i, j: (0, j)),
                  pl.BlockSpec((tm, tn), lambda i, j: (i, j)),
                  pl.BlockSpec((1, 6, tn), lambda i, j: (i // per_b, 0, j))],
        out_specs=pl.BlockSpec((tm, tn), lambda i, j: (i, j)),
        out_shape=jax.ShapeDtypeStruct((m, n), F32),
        compiler_params=_cp(("parallel", "parallel")),
        name="out_proj",
    )(y, w, x2, mod)


QKV_TILE = 4 * LANE
HALF_NORM_TILES = (0, 1)
FULL_NORM_TILES = (6, 7, 9)
LAST_TILE = 11


def _qkv_kernel(a_ref, b_ref, g_ref, q_ref, w_ref):
    j = pl.program_id(1)
    tm = a_ref.shape[0]
    y = _dot(a_ref[...], b_ref[...])
    gain = g_ref[pl.ds(j, 1), :]
    lane = lax.broadcasted_iota(I32, (tm, LANE), 1)
    low = lane < (LANE // 2)

    def blk(c):
        return y[:, c * LANE:(c + 1) * LANE]

    def put(c, v):
        q_ref[:, c * LANE:(c + 1) * LANE] = v.astype(BF16)

    def norm_full(v):
        ms = jnp.mean(v * v, axis=-1, keepdims=True)
        return v * lax.rsqrt(ms + RMS_EPS) * gain

    def norm_half(v):
        vv = v * v
        s_lo = jnp.sum(jnp.where(low, vv, 0.0), axis=-1, keepdims=True)
        s_hi = jnp.sum(jnp.where(low, 0.0, vv), axis=-1, keepdims=True)
        ms = jnp.where(low, s_lo, s_hi) * (2.0 / LANE)
        return v * lax.rsqrt(ms + RMS_EPS) * gain

    is_half = functools.reduce(jnp.logical_or, [j == n for n in HALF_NORM_TILES])
    is_full = functools.reduce(jnp.logical_or, [j == n for n in FULL_NORM_TILES])
    is_last = j == LAST_TILE

    @pl.when(is_half)
    def _():
        for c in range(QKV_TILE // LANE):
            put(c, norm_half(blk(c)))

    @pl.when(is_full)
    def _():
        for c in range(QKV_TILE // LANE):
            put(c, norm_full(blk(c)))

    @pl.when(is_last)
    def _():
        put(0, norm_full(blk(0)))
        put(1, blk(1))
        ik = blk(2)
        w_ref[...] = ik
        put(2, jnp.where(low, ik, pltpu.roll(ik, LANE // 2, 1)))
        put(3, blk(3))

    @pl.when(jnp.logical_not(is_half | is_full | is_last))
    def _():
        for c in range(QKV_TILE // LANE):
            put(c, blk(c) * gain)


def _qkv_proj(h, w_qkv, gains):
    t, k = h.shape
    tm = min(1024, t)
    assert DK_BLOCK * LANE == LAST_TILE * QKV_TILE and IK_BLOCK == DK_BLOCK + 2
    return pl.pallas_call(
        _qkv_kernel,
        grid=(t // tm, QKV_PAD // QKV_TILE),
        in_specs=[pl.BlockSpec((tm, k), lambda i, j: (i, 0)),
                  pl.BlockSpec((k, QKV_TILE), lambda i, j: (0, j)),
                  pl.BlockSpec(gains.shape, lambda i, j: (0, 0))],
        out_specs=[pl.BlockSpec((tm, QKV_TILE), lambda i, j: (i, j)),
                   pl.BlockSpec((tm, LANE), lambda i, j: (i, 0))],
        out_shape=[jax.ShapeDtypeStruct((t, QKV_PAD), BF16),
                   jax.ShapeDtypeStruct((t, LANE), F32)],
        compiler_params=_cp(("parallel", "arbitrary")),
        name="in_proj_qkv",
    )(h, w_qkv, gains)


ATT_TILE = 256


def _dot_tn(a, b):
    return lax.dot_general(a, b, (((0,), (0,)), ((), ())), preferred_element_type=F32)


def _flash_init(s, v, m_ref, l_ref, acc_ref, h):
    m = jnp.max(s, axis=0, keepdims=True)
    p = jnp.exp(s - m)
    m_ref[h] = m
    l_ref[h] = jnp.sum(p, axis=0, keepdims=True)
    acc_ref[h] = _dot_tn(v, p.astype(BF16))


def _flash_step(s, v, m_ref, l_ref, acc_ref, h, keep=None):
    m = m_ref[h]
    m_new = jnp.maximum(m, jnp.max(s, axis=0, keepdims=True))
    alpha = jnp.exp(m - m_new)
    p = jnp.exp(s - m_new)
    if keep is not None:
        p = jnp.where(keep, p, 0.0)
    l_ref[h] = alpha * l_ref[h] + jnp.sum(p, axis=0, keepdims=True)
    acc_ref[h] = alpha * acc_ref[h] + _dot_tn(v, p.astype(BF16))
    m_ref[h] = m_new


def _head_cols(h):
    return slice(h * LANE, (h + 1) * LANE)


def _diff_kernel(sc_ref, q_ref, k_ref, v_ref, g_ref, o_ref, q12_ref, m_ref, l_ref, acc_ref, *, t, slopes):
    i = pl.program_id(1)
    lam = sc_ref[0]
    post = sc_ref[1]
    lane = lax.broadcasted_iota(I32, (t, LANE), 1)
    low = lane < (LANE // 2)
    causal = lax.broadcasted_iota(I32, (t, t), 0) <= lax.broadcasted_iota(I32, (t, t), 1)
    kpos = lax.broadcasted_iota(I32, (t, 1), 0).astype(F32)
    off_d = pl.multiple_of(i * t, t)
    for h in range(A_HEADS):
        cs = _head_cols(h)
        q = q_ref[:, cs]
        zero = jnp.zeros_like(q)
        q12_ref[h] = jnp.concatenate([jnp.where(low, q, zero), jnp.where(low, zero, q)], axis=0)
        bias = jnp.where(causal, slopes[h] * kpos, NEG_BIG)
        s = _dot_t(k_ref[pl.ds(off_d, t), cs], q12_ref[h]) + jnp.concatenate([bias, bias], axis=1)
        _flash_init(s, v_ref[pl.ds(off_d, t), cs], m_ref, l_ref, acc_ref, h)

    def body(kt, _):
        off = pl.multiple_of(kt * t, t)
        shift = ((kt - i) * t).astype(F32)
        for h in range(A_HEADS):
            cs = _head_cols(h)
            s = _dot_t(k_ref[pl.ds(off, t), cs], q12_ref[h]) + slopes[h] * (kpos + shift)
            _flash_step(s, v_ref[pl.ds(off, t), cs], m_ref, l_ref, acc_ref, h)
        return 0

    lax.fori_loop(0, i, body, 0)
    for h in range(A_HEADS):
        o = acc_ref[h] / l_ref[h]
        o = o[:, :t] - lam * o[:, t:]
        ms = jnp.mean(o * o, axis=0, keepdims=True)
        o = o * lax.rsqrt(ms + RMS_EPS) * g_ref[...] * post
        o_ref[:, _head_cols(h)] = o.T.astype(BF16)


def _diff_attention(qkv, scal, subln_g, bsz, seq, slopes):
    t = ATT_TILE
    nq = seq // t
    hw = A_HEADS * LANE
    return pl.pallas_call(
        functools.partial(_diff_kernel, t=t, slopes=slopes),
        grid=(bsz, nq),
        in_specs=[pl.BlockSpec(memory_space=pltpu.SMEM),
                  pl.BlockSpec((t, hw), lambda b, i: (b * nq + i, 0)),
                  pl.BlockSpec((seq, hw), lambda b, i: (b, 1)),
                  pl.BlockSpec((seq, hw), lambda b, i: (b, 2)),
                  pl.BlockSpec((LANE, 1), lambda b, i: (0, 0))],
        out_specs=pl.BlockSpec((t, hw), lambda b, i: (b * nq + i, 0)),
        out_shape=jax.ShapeDtypeStruct((bsz * seq, hw), BF16),
        scratch_shapes=[pltpu.VMEM((A_HEADS, 2 * t, LANE), BF16),
                        pltpu.VMEM((A_HEADS, 1, 2 * t), F32),
                        pltpu.VMEM((A_HEADS, 1, 2 * t), F32),
                        pltpu.VMEM((A_HEADS, LANE, 2 * t), F32)],
        compiler_params=_cp(("parallel", "parallel")),
        name="diff_attn",
    )(scal, qkv, qkv, qkv, subln_g.reshape(LANE, 1))


def _stick_kernel(q_ref, k_ref, v_ref, o_ref, tail_ref, acc_ref, *, t):
    i = pl.program_id(1)
    r = lax.broadcasted_iota(I32, (t, t), 0)
    c = lax.broadcasted_iota(I32, (t, t), 1)
    strict = r < c
    later = jnp.where(c > r, 1.0, 0.0).astype(BF16)

    def tile(off, h, masked):
        cs = _head_cols(h)
        z = _dot_t(k_ref[pl.ds(off, t), cs], q_ref[:, cs])
        ls = jnp.minimum(z, 0.0) - jnp.log1p(jnp.exp(-jnp.abs(z)))
        l1 = ls - z
        if masked:
            l1 = jnp.where(strict, l1, 0.0)
        hi, lo = _split(l1)
        after = _dot(later, hi) + _dot(later, lo) + tail_ref[h]
        w = jnp.exp(ls + after)
        if masked:
            w = jnp.where(strict, w, 0.0)
        acc_ref[h] = acc_ref[h] + _dot_tn(v_ref[pl.ds(off, t), cs], w.astype(BF16))
        tail_ref[h] = tail_ref[h] + jnp.sum(l1, axis=0, keepdims=True)

    tail_ref[...] = jnp.zeros_like(tail_ref)
    acc_ref[...] = jnp.zeros_like(acc_ref)
    off_d = pl.multiple_of(i * t, t)
    for h in range(B_HEADS):
        tile(off_d, h, True)

    def body(j, _):
        off = pl.multiple_of((i - 1 - j) * t, t)
        for h in range(B_HEADS):
            tile(off, h, False)
        return 0

    lax.fori_loop(0, i, body, 0)
    for h in range(B_HEADS):
        o_ref[:, _head_cols(h)] = acc_ref[h].T.astype(BF16)


def _stick_attention(qkv, bsz, seq):
    t = ATT_TILE
    nq = seq // t
    hw = B_HEADS * LANE
    return pl.pallas_call(
        functools.partial(_stick_kernel, t=t),
        grid=(bsz, nq),
        in_specs=[pl.BlockSpec((t, hw), lambda b, i: (b * nq + i, 3)),
                  pl.BlockSpec((seq, hw), lambda b, i: (b, 4)),
                  pl.BlockSpec((seq, hw), lambda b, i: (b, 5))],
        out_specs=pl.BlockSpec((t, hw), lambda b, i: (b * nq + i, 0)),
        out_shape=jax.ShapeDtypeStruct((bsz * seq, hw), BF16),
        scratch_shapes=[pltpu.VMEM((B_HEADS, 1, t), F32),
                        pltpu.VMEM((B_HEADS, LANE, t), F32)],
        compiler_params=_cp(("parallel", "parallel")),
        name="stick_attn",
    )(qkv, qkv, qkv)


def _moba_kernel(q_ref, k_ref, v_ref, o_ref, sel_ref, m_ref, l_ref, acc_ref, *, t, n_kb, n_sel, slopes):
    i = pl.program_id(1)
    seq = k_ref.shape[0]
    blk = lax.broadcasted_iota(I32, (LANE, t), 0)
    causal = lax.broadcasted_iota(I32, (t, t), 0) <= lax.broadcasted_iota(I32, (t, t), 1)
    kpos = lax.broadcasted_iota(I32, (t, 1), 0).astype(F32)
    lo_edge = lax.broadcasted_iota(I32, (LANE, seq), 0) * t
    bc = lax.broadcasted_iota(I32, (LANE, seq), 1)
    avg = jnp.where((bc >= lo_edge) & (bc < lo_edge + t), 1.0 / t, 0.0).astype(BF16)
    off_d = pl.multiple_of(i * t, t)
    for h in range(C_HEADS):
        cs = _head_cols(h)
        q = q_ref[:, cs]
        k_mean = _dot(avg, k_ref[:, cs]).astype(BF16)
        gate = jnp.where(blk < i, _dot_t(k_mean, q), -jnp.inf)
        rank = jnp.zeros((LANE, t), F32)
        for mth in range(n_kb):
            gm = gate[mth:mth + 1, :]
            beats = (gm > gate) | ((gm == gate) & (blk > mth))
            rank = rank + jnp.where(beats, 1.0, 0.0)
        sel_ref[h] = jnp.where((rank < n_sel) & (blk < i), 1.0, 0.0)
        bias = jnp.where(causal, slopes[h] * kpos, NEG_BIG)
        s = _dot_t(k_ref[pl.ds(off_d, t), cs], q) + bias
        _flash_init(s, v_ref[pl.ds(off_d, t), cs], m_ref, l_ref, acc_ref, h)

    def body(kt, _):
        off = pl.multiple_of(kt * t, t)
        shift = ((kt - i) * t).astype(F32)
        for h in range(C_HEADS):
            cs = _head_cols(h)
            keep = jnp.broadcast_to(sel_ref[h, pl.ds(kt, 1), :], (t, t)) > 0.0
            s = _dot_t(k_ref[pl.ds(off, t), cs], q_ref[:, cs]) + slopes[h] * (kpos + shift)
            s = jnp.where(keep, s, NEG_BIG)
            _flash_step(s, v_ref[pl.ds(off, t), cs], m_ref, l_ref, acc_ref, h, keep=keep)
        return 0

    lax.fori_loop(0, i, body, 0)
    for h in range(C_HEADS):
        o_ref[:, _head_cols(h)] = (acc_ref[h] / l_ref[h]).T.astype(BF16)


def _moba_attention(qkv, bsz, seq, slopes):
    t = MOBA_BLOCK
    assert seq % t == 0 and t == ATT_TILE
    n_kb = seq // t
    assert n_kb <= LANE
    n_sel = min(MOBA_TOPK, n_kb - 1)
    hw = C_HEADS * LANE
    return pl.pallas_call(
        functools.partial(_moba_kernel, t=t, n_kb=n_kb, n_sel=n_sel, slopes=slopes),
        grid=(bsz, n_kb),
        in_specs=[pl.BlockSpec((t, hw), lambda b, i: (b * n_kb + i, 6)),
                  pl.BlockSpec((seq, hw), lambda b, i: (b, 7)),
                  pl.BlockSpec((seq, hw), lambda b, i: (b, 8))],
        out_specs=pl.BlockSpec((t, hw), lambda b, i: (b * n_kb + i, 0)),
        out_shape=jax.ShapeDtypeStruct((bsz * seq, hw), BF16),
        scratch_shapes=[pltpu.VMEM((C_HEADS, LANE, t), F32),
                        pltpu.VMEM((C_HEADS, 1, t), F32),
                        pltpu.VMEM((C_HEADS, 1, t), F32),
                        pltpu.VMEM((C_HEADS, LANE, t), F32)],
        compiler_params=_cp(("parallel", "parallel")),
        name="moba_attn",
    )(qkv, qkv, qkv)


def _dsa_kernel(q_ref, k_ref, v_ref, iq_ref, ki_ref, w_ref, o_ref,
                keys_ref, iqm_ref, cut_ref, m_ref, l_ref, acc_ref, *, t, n_keep, idx_bits, slopes):
    i = pl.program_id(1)
    n_kt = i + 1
    lane = lax.broadcasted_iota(I32, (t, LANE), 1)
    low = lane < (LANE // 2)
    kidx = lax.broadcasted_iota(I32, (t, t), 0)
    qpos = i * t + lax.broadcasted_iota(I32, (t, t), 1)
    kpos = lax.broadcasted_iota(I32, (t, 1), 0).astype(F32)
    w_scale = (IDX_DH ** -0.5) * (IDX_HEADS ** -0.5)
    w_t = w_ref[...].T * w_scale
    for p in range(IDX_HEADS // 2):
        qp = iq_ref[:, p * LANE:(p + 1) * LANE]
        zero = jnp.zeros_like(qp)
        iqm_ref[2 * p] = jnp.where(low, qp, zero)
        iqm_ref[2 * p + 1] = jnp.where(low, zero, qp)

    def score_body(kt, _):
        off = pl.multiple_of(kt * t, t)
        kk = ki_ref[pl.ds(off, t), :]
        sc = jnp.zeros((t, t), F32)
        for j in range(IDX_HEADS):
            wj = w_t[LANE // 2 + j:LANE // 2 + j + 1, :]
            sc = sc + jnp.maximum(_dot_t(kk, iqm_ref[j]), 0.0) * wj
        sc = jnp.where(sc == 0.0, 0.0, sc)
        bits = lax.bitcast_convert_type(sc, I32)
        key = jnp.bitwise_xor(bits, jnp.bitwise_and(lax.shift_right_arithmetic(bits, 31), 0x7FFFFFFF))
        keys_ref[pl.ds(off, t), :] = jnp.where(off + kidx <= qpos, key, INT_MIN)
        return 0

    lax.fori_loop(0, n_kt, score_body, 0)

    def count(pred):
        def body(kt, acc):
            off = pl.multiple_of(kt * t, t)
            hit = jnp.where(pred(keys_ref[pl.ds(off, t), :], off + kidx), 1.0, 0.0)
            return acc + jnp.sum(hit, axis=0, keepdims=True)
        return lax.fori_loop(0, n_kt, body, jnp.zeros((1, t), F32))

    keep_f = float(n_keep)
    zero_t = jnp.zeros((1, t), I32)
    c0 = count(lambda kc, col: kc >= zero_t)
    thr = jnp.where(c0 >= keep_f, 0, INT_MIN).astype(I32)
    n_ge = jnp.where(c0 >= keep_f, c0, (n_kt * t).astype(F32))

    def bit_cond(state):
        b, _, n_ge = state
        return (b < 31) & (jnp.max(jnp.abs(n_ge - keep_f)) > 0.0)

    def bit_body(state):
        b, thr, n_ge = state
        cand = jnp.bitwise_or(thr, lax.shift_left(jnp.int32(1), 30 - b))
        cnt = count(lambda kc, col: kc >= cand)
        take = cnt >= keep_f
        return b + 1, jnp.where(take, cand, thr), jnp.where(take, cnt, n_ge)

    _, thr, n_ge = lax.while_loop(bit_cond, bit_body, (jnp.int32(0), thr, n_ge))

    need = (n_ge > keep_f) & (thr > INT_MIN)
    big = jnp.int32(2 ** 30)
    cut_ref[...] = jnp.full((1, t), big, I32)

    @pl.when(jnp.max(jnp.where(need, 1.0, 0.0)) > 0.0)
    def _():
        quota = keep_f - count(lambda kc, col: kc > thr)

        def tie_body(b, cut):
            cand = jnp.bitwise_or(cut, lax.shift_left(jnp.int32(1), idx_bits - 1 - b))
            cnt = count(lambda kc, col: (kc == thr) & (col < cand))
            return jnp.where(cnt <= quota, cand, cut)
        cut = lax.fori_loop(0, idx_bits, tie_body, jnp.zeros((1, t), I32))
        cut_ref[...] = jnp.where(need, cut, big)

    cut = cut_ref[...]

    def flag_body(kt, _):
        off = pl.multiple_of(kt * t, t)
        kc = keys_ref[pl.ds(off, t), :]
        col = off + kidx
        sel = ((kc > thr) | ((kc == thr) & (col < cut))) & (col <= qpos)
        keys_ref[pl.ds(off, t), :] = jnp.where(sel, 1, 0).astype(I32)
        return 0

    lax.fori_loop(0, n_kt, flag_body, 0)

    m_ref[...] = jnp.full(m_ref.shape, NEG_BIG, F32)
    l_ref[...] = jnp.zeros_like(l_ref)
    acc_ref[...] = jnp.zeros_like(acc_ref)

    def body(kt, _):
        off = pl.multiple_of(kt * t, t)
        shift = ((kt - i) * t).astype(F32)
        keep = keys_ref[pl.ds(off, t), :] > 0
        k = k_ref[pl.ds(off, t), :]
        v = v_ref[pl.ds(off, t), :]
        for h in range(D_HEADS):
            s = _dot_t(k, q_ref[:, _head_cols(h)]) + slopes[h] * (kpos + shift)
            s = jnp.where(keep, s, NEG_BIG)
            _flash_step(s, v, m_ref, l_ref, acc_ref, h, keep=keep)
        return 0

    lax.fori_loop(0, n_kt, body, 0)
    for h in range(D_HEADS):
        o_ref[:, _head_cols(h)] = (acc_ref[h] / l_ref[h]).T.astype(BF16)


def _dsa_attention(qkv, wraw, bsz, seq, slopes):
    t = ATT_TILE
    nq = seq // t
    n_keep = min(DSA_TOPK, seq // 4)
    idx_bits = int(math.ceil(math.log2(seq))) + 1
    hw = D_HEADS * LANE
    iw = IDX_HEADS * IDX_DH
    return pl.pallas_call(
        functools.partial(_dsa_kernel, t=t, n_keep=n_keep, idx_bits=idx_bits, slopes=slopes),
        grid=(bsz, nq),
        in_specs=[pl.BlockSpec((t, hw), lambda b, i: (b * nq + i, 9)),
                  pl.BlockSpec((seq, LANE), lambda b, i: (b, DK_BLOCK)),
                  pl.BlockSpec((seq, LANE), lambda b, i: (b, DV_BLOCK)),
                  pl.BlockSpec((t, iw), lambda b, i: (b * nq + i, IQ_BLOCK * LANE // iw)),
                  pl.BlockSpec((seq, LANE), lambda b, i: (b, IK_BLOCK)),
                  pl.BlockSpec((t, LANE), lambda b, i: (b * nq + i, 0))],
        out_specs=pl.BlockSpec((t, hw), lambda b, i: (b * nq + i, 0)),
        out_shape=jax.ShapeDtypeStruct((bsz * seq, hw), BF16),
        scratch_shapes=[pltpu.VMEM((seq, t), I32),
                        pltpu.VMEM((IDX_HEADS, t, LANE), BF16),
                        pltpu.VMEM((1, t), I32),
                        pltpu.VMEM((D_HEADS, 1, t), F32),
                        pltpu.VMEM((D_HEADS, 1, t), F32),
                        pltpu.VMEM((D_HEADS, LANE, t), F32)],
        compiler_params=_cp(("parallel", "parallel")),
        name="dsa_attn",
    )(qkv, qkv, qkv, qkv, qkv, wraw)


def _merge_kernel(a_ref, b_ref, c_ref, d_ref, wb_ref, g0, g1, g2, g3, gb_ref, y_ref):
    acc = None
    for n, (br, gl) in enumerate(zip((a_ref, b_ref, c_ref, d_ref), (g0, g1, g2, g3))):
        up = _dot(br[...], wb_ref[n])
        gate = jax.nn.sigmoid(gl[...].astype(F32) + gb_ref[n])
        acc = gate * up if acc is None else acc + gate * up
    y_ref[...] = acc.astype(BF16)


def _merge(branches, wb, gl, gate_b):
    t, bw = branches[0].shape
    d = wb.shape[2]
    tm = min(1024, t)
    tn = 512
    nj = d // tn
    br_spec = pl.BlockSpec((tm, bw), lambda j, i: (i, 0))
    gl_specs = [pl.BlockSpec((tm, tn), functools.partial(lambda j, i, n: (i, n * nj + j), n=n))
                for n in range(N_BRANCH)]
    return pl.pallas_call(
        _merge_kernel,
        grid=(nj, t // tm),
        in_specs=[br_spec] * 4 + [pl.BlockSpec((N_BRANCH, bw, tn), lambda j, i: (0, 0, j))] + gl_specs
                 + [pl.BlockSpec((N_BRANCH, 1, tn), lambda j, i: (0, 0, j))],
        out_specs=pl.BlockSpec((tm, tn), lambda j, i: (i, j)),
        out_shape=jax.ShapeDtypeStruct((t, d), BF16),
        compiler_params=_cp(("parallel", "parallel")),
        name="merge",
    )(*branches, wb, gl, gl, gl, gl, gate_b.reshape(N_BRANCH, 1, d))


def _gather_rows_kernel(tok_ref, live_ref, src_ref, out_ref, sem, *, rows):
    g = pl.program_id(0)
    n_sub = rows // MOE_SUB
    for c in range(n_sub):
        live = live_ref[g * n_sub + c] > 0
        lo = c * MOE_SUB

        @pl.when(live)
        def _(lo=lo):
            def issue(r, _):
                pltpu.make_async_copy(src_ref.at[pl.ds(tok_ref[g * rows + lo + r], 1)],
                                      out_ref.at[pl.ds(lo + r, 1)], sem).start()
                return 0
            lax.fori_loop(0, MOE_SUB, issue, 0, unroll=8)

        @pl.when(jnp.logical_not(live))
        def _(lo=lo):
            out_ref[lo:lo + MOE_SUB, :] = jnp.zeros((MOE_SUB, out_ref.shape[1]), out_ref.dtype)

    for c in range(n_sub):
        lo = c * MOE_SUB

        @pl.when(live_ref[g * n_sub + c] > 0)
        def _(lo=lo):
            pltpu.make_async_copy(src_ref.at[pl.ds(0, MOE_SUB)], out_ref.at[pl.ds(lo, MOE_SUB)], sem).wait()


def _gather_rows(src, row_tok, sub_live):
    n_rows = row_tok.shape[0]
    rows = max(r for r in (4 * MOE_SUB, 2 * MOE_SUB) if n_rows % r == 0)
    return pl.pallas_call(
        functools.partial(_gather_rows_kernel, rows=rows),
        grid_spec=pltpu.PrefetchScalarGridSpec(
            num_scalar_prefetch=2,
            grid=(n_rows // rows,),
            in_specs=[pl.BlockSpec(memory_space=pl.ANY)],
            out_specs=pl.BlockSpec((rows, src.shape[1]), lambda i, tok, lv: (i, 0)),
            scratch_shapes=[pltpu.SemaphoreType.DMA]),
        out_shape=jax.ShapeDtypeStruct((n_rows, src.shape[1]), src.dtype),
        compiler_params=_cp(("arbitrary",)),
        name="moe_gather",
    )(row_tok, sub_live, src)


def _unpack_rows(xp):
    lo = lax.bitcast_convert_type(lax.shift_left(xp, 16), F32).astype(BF16)
    hi = lax.bitcast_convert_type(jnp.bitwise_and(xp, -65536), F32).astype(BF16)
    return lo, hi


def _expert_changed(be_ref, i):
    return (i == 0) | (be_ref[i] != be_ref[jnp.maximum(i - 1, 0)])


def _live_rows_dispatch(n_live, out_ref, compute):
    for n in range(1, MOE_BLK // MOE_SUB + 1):
        @pl.when(n_live == n)
        def _(n=n):
            rows = n * MOE_SUB
            compute(rows)
            if rows < MOE_BLK:
                out_ref[rows:, :] = jnp.zeros((MOE_BLK - rows, out_ref.shape[1]), out_ref.dtype)

    @pl.when(n_live == 0)
    def _():
        out_ref[...] = jnp.zeros_like(out_ref)


def _expert_up_kernel(be_ref, bl_ref, nu_ref, xp_ref, wg_ref, wl_ref, bg_ref, bb_ref, act_ref, wg_s, wl_s):
    i = pl.program_id(1)
    n_live = bl_ref[i]

    @pl.when((n_live > 0) & _expert_changed(be_ref, i))
    def _():
        wg_s[...] = wg_ref[0, 0].astype(BF16)
        wl_s[...] = wl_ref[0, 0].astype(BF16)

    def compute(rows):
        half = xp_ref.shape[1]
        lo, hi = _unpack_rows(xp_ref[0:rows, :])
        glu = _dot(lo, wg_s[:half, :]) + _dot(hi, wg_s[half:, :]) + bg_ref[0]
        lin = _dot(lo, wl_s[:half, :]) + _dot(hi, wl_s[half:, :]) + bb_ref[0]
        glu = jnp.minimum(glu, SWIGLU_LIMIT)
        lin = jnp.clip(lin, -SWIGLU_LIMIT, SWIGLU_LIMIT)
        act = glu * jax.nn.sigmoid(SWIGLU_ALPHA * glu) * (lin + 1.0)
        act_ref[0:rows, :] = act.astype(BF16)

    _live_rows_dispatch(n_live, act_ref, compute)


def _expert_up(block_exp, block_live, n_used, xs, w1, b1, d_ff, l):
    n_rows, half = xs.shape
    d = 2 * half
    mb = MOE_BLK
    tf = 512
    nf = d_ff // tf
    n_blocks = n_rows // mb
    return pl.pallas_call(
        _expert_up_kernel,
        grid_spec=pltpu.PrefetchScalarGridSpec(
            num_scalar_prefetch=3,
            grid=(nf, n_blocks),
            in_specs=[pl.BlockSpec((mb, half), lambda j, i, be, bl, nu: (jnp.minimum(i, nu[0] - 1), 0)),
                      pl.BlockSpec((1, 1, d, tf), lambda j, i, be, bl, nu: (l, be[i], 0, j)),
                      pl.BlockSpec((1, 1, d, tf), lambda j, i, be, bl, nu: (l, be[i], 0, nf + j)),
                      pl.BlockSpec((1, 1, tf), lambda j, i, be, bl, nu: (be[i], 0, j)),
                      pl.BlockSpec((1, 1, tf), lambda j, i, be, bl, nu: (be[i], 0, nf + j))],
            out_specs=pl.BlockSpec((mb, tf), lambda j, i, be, bl, nu: (i, j)),
            scratch_shapes=[pltpu.VMEM((d, tf), BF16), pltpu.VMEM((d, tf), BF16)]),
        out_shape=jax.ShapeDtypeStruct((n_rows, d_ff), BF16),
        compiler_params=_cp(("arbitrary", "arbitrary")),
        name="expert_up",
    )(block_exp, block_live, n_used, xs, w1, w1, b1, b1)


def _expert_down_kernel(be_ref, bl_ref, nu_ref, a_ref, w_ref, b_ref, y_ref, w_s):
    i = pl.program_id(1)
    n_live = bl_ref[i]

    @pl.when((n_live > 0) & _expert_changed(be_ref, i))
    def _():
        w_s[...] = w_ref[0, 0].astype(BF16)

    def compute(rows):
        y_ref[0:rows, :] = _dot(a_ref[0:rows, :], w_s[...]) + b_ref[0]

    _live_rows_dispatch(n_live, y_ref, compute)


def _expert_down(block_exp, block_live, n_used, act, w2, b2, l):
    n_rows, d_ff = act.shape
    d = w2.shape[3]
    mb = MOE_BLK
    tn = 1024
    nn = d // tn
    n_blocks = n_rows // mb
    return pl.pallas_call(
        _expert_down_kernel,
        grid_spec=pltpu.PrefetchScalarGridSpec(
            num_scalar_prefetch=3,
            grid=(nn, n_blocks),
            in_specs=[pl.BlockSpec((mb, d_ff), lambda j, i, be, bl, nu: (jnp.minimum(i, nu[0] - 1), 0)),
                      pl.BlockSpec((1, 1, d_ff, tn), lambda j, i, be, bl, nu: (l, be[i], 0, j)),
                      pl.BlockSpec((1, 1, tn), lambda j, i, be, bl, nu: (be[i], 0, j))],
            out_specs=pl.BlockSpec((mb, tn), lambda j, i, be, bl, nu: (i, j)),
            scratch_shapes=[pltpu.VMEM((d_ff, tn), BF16)]),
        out_shape=jax.ShapeDtypeStruct((n_rows, d), F32),
        compiler_params=_cp(("arbitrary", "arbitrary")),
        name="expert_down",
    )(block_exp, block_live, n_used, act, w2, b2)


def _combine_kernel(dest_ref, y_ref, x_ref, gate_ref, mod_ref, o_ref, buf, sem, *, tm):
    base = pl.program_id(0) * tm * TOP_K

    def issue(r, _):
        for k in range(TOP_K):
            pltpu.make_async_copy(y_ref.at[pl.ds(dest_ref[base + r * TOP_K + k], 1)],
                                  buf.at[pl.ds(k * tm + r, 1)], sem).start()
        return 0

    lax.fori_loop(0, tm, issue, 0, unroll=4)
    pltpu.make_async_copy(y_ref.at[pl.ds(0, TOP_K * tm)], buf, sem).wait()
    gate = gate_ref[...]
    acc = gate[:, 0:1] * buf[0:tm, :]
    for k in range(1, TOP_K):
        acc = acc + gate[:, k:k + 1] * buf[k * tm:(k + 1) * tm, :]
    o_ref[...] = x_ref[...] + mod_ref[0][5:6] * acc


def _combine(dest, y, x2, gate, mod, seq):
    t, d = x2.shape
    tm = min(256, seq)
    per_b = seq // tm
    return pl.pallas_call(
        functools.partial(_combine_kernel, tm=tm),
        grid_spec=pltpu.PrefetchScalarGridSpec(
            num_scalar_prefetch=1,
            grid=(t // tm,),
            in_specs=[pl.BlockSpec(memory_space=pl.ANY),
                      pl.BlockSpec((tm, d), lambda i, dst: (i, 0)),
                      pl.BlockSpec((tm, LANE), lambda i, dst: (i, 0)),
                      pl.BlockSpec((1, 6, d), lambda i, dst: (i // per_b, 0, 0))],
            out_specs=pl.BlockSpec((tm, d), lambda i, dst: (i, 0)),
            scratch_shapes=[pltpu.VMEM((TOP_K * tm, d), F32), pltpu.SemaphoreType.DMA]),
        out_shape=jax.ShapeDtypeStruct((t, d), F32),
        compiler_params=_cp(("arbitrary",)),
        name="moe_combine",
    )(dest, y, x2, gate, mod)


def _moe(x2, g, mod, rw, rb, w1, b1, w2, b2, seq, l):
    t, d = x2.shape
    _, n_exp, _, two_f = w1.shape
    d_ff = two_f // 2
    rw_p = jnp.pad(rw, ((0, 0), (0, LANE - n_exp))).astype(BF16)
    rb_p = jnp.pad(rb, (0, LANE - n_exp)).reshape(1, LANE)
    hp, idx, gate, rank, cnt = _router(x2, g, mod, rw_p, rb_p, seq, n_exp)

    per_blk = MOE_BLK // MOE_SUB
    counts = cnt[0, :n_exp]
    live_sub = (counts + MOE_SUB - 1) // MOE_SUB
    n_blk = (counts + MOE_BLK - 1) // MOE_BLK
    b_end = jnp.cumsum(n_blk)
    b_start = b_end - n_blk
    top_idx = idx[:, :TOP_K]
    dest = (b_start[top_idx] * MOE_BLK + rank[:, :TOP_K]).reshape(-1).astype(I32)
    n_assign = t * TOP_K
    n_blocks = -(-(n_assign + n_exp * (MOE_BLK - 1)) // MOE_BLK)
    n_rows = n_blocks * MOE_BLK
    tok = (jnp.arange(n_assign, dtype=I32) // TOP_K)
    row_tok = jnp.zeros((n_rows,), I32).at[dest].set(tok)
    blocks = jnp.arange(n_blocks, dtype=I32)
    block_exp = jnp.minimum(jnp.sum(b_end[None, :] <= blocks[:, None], axis=1), n_exp - 1).astype(I32)
    block_live = jnp.clip(live_sub[block_exp] - per_blk * (blocks - b_start[block_exp]), 0, per_blk).astype(I32)
    subs = jnp.arange(n_blocks * per_blk, dtype=I32)
    sub_live = (subs % per_blk < block_live[subs // per_blk]).astype(I32)
    n_used = b_end[-1:].astype(I32)

    xs = _gather_rows(hp, row_tok, sub_live)
    act = _expert_up(block_exp, block_live, n_used, xs, w1, b1.reshape(n_exp, 1, two_f), d_ff, l)
    y = _expert_down(block_exp, block_live, n_used, act, w2, b2.reshape(n_exp, 1, d), l)
    return _combine(dest, y, x2, gate, mod, seq)


def _token_mixer(x2, l, mod, bsz, seq, norm1_g, w_in, gate_b, a_qn_g, a_kn_g, a_lam_q1, a_lam_k1,
                 a_lam_q2, a_lam_k2, a_subln_g, c_qn_g, c_kn_g, d_qn_g, d_kn_g, w_branch, w_out):
    d = x2.shape[1]
    slopes = _alibi_slopes()
    h = _norm1(x2, norm1_g, mod, seq)
    dq_end = DK_BLOCK * LANE - IDX_HEADS * IDX_DH
    iq_lo = dq_end + 2 * D_DH
    iq_hi = iq_lo + IDX_HEADS * IDX_DH
    w_qkv = jnp.concatenate([w_in[:, :dq_end], w_in[:, iq_lo:iq_hi], w_in[:, dq_end:iq_lo],
                             w_in[:, iq_hi:QKV_COLS], jnp.zeros((d, QKV_PAD - QKV_COLS), F32)],
                            axis=1).astype(BF16)
    w_gl = w_in[:, QKV_COLS:].astype(BF16)
    gl = _matmul(h, w_gl, BF16, 1024, 512, "in_proj_gate")

    ones = jnp.ones((LANE,), F32)
    gains = jnp.stack([
        jnp.tile(a_qn_g, 2) * (A_DH ** -0.5),
        jnp.tile(a_kn_g, 2),
        ones,
        ones * (B_DH ** -0.5),
        ones, ones,
        c_qn_g * (C_DH ** -0.5),
        c_kn_g,
        ones,
        d_qn_g * (D_DH ** -0.5),
        ones,
        d_kn_g,
        ones, ones, ones, ones]).astype(F32)
    qkv, wraw = _qkv_proj(h, w_qkv, gains)

    lambda_init = 0.8 - 0.6 * math.exp(-0.3 * l)
    lam = (jnp.exp(jnp.sum(a_lam_q1 * a_lam_k1)) - jnp.exp(jnp.sum(a_lam_q2 * a_lam_k2)) + lambda_init)
    scal = jnp.stack([lam, jnp.float32(1.0 - lambda_init)]).astype(F32)

    oa = _diff_attention(qkv, scal, a_subln_g, bsz, seq, slopes[0::3])
    ob = _stick_attention(qkv, bsz, seq)
    oc = _moba_attention(qkv, bsz, seq, slopes[1::3])
    od = _dsa_attention(qkv, wraw, bsz, seq, slopes[2::3])

    y = _merge([oa, ob, oc, od], w_branch.astype(BF16), gl, gate_b)
    return _out_proj(y, w_out.astype(BF16), x2, mod, seq)


def kernel(x, c, ada_w, ada_b, norm1_g, norm2_g, w_in, gate_b, a_qn_g, a_kn_g, a_lam_q1, a_lam_k1,
           a_lam_q2, a_lam_k2, a_subln_g, c_qn_g, c_kn_g, d_qn_g, d_kn_g, w_branch, w_out,
           router_w, router_b, w1, b1, w2, b2):
    bsz, seq, d = x.shape
    depth = ada_w.shape[0]
    mods = _ada_mod(c, ada_w, ada_b)
    x2 = x.reshape(bsz * seq, d)
    for l in range(depth):
        mod = mods[l]
        x2 = _token_mixer(x2, l, mod, bsz, seq, norm1_g[l], w_in[l], gate_b[l], a_qn_g[l], a_kn_g[l],
                          a_lam_q1[l], a_lam_k1[l], a_lam_q2[l], a_lam_k2[l], a_subln_g[l],
                          c_qn_g[l], c_kn_g[l], d_qn_g[l], d_kn_g[l], w_branch[l], w_out[l])
        x2 = _moe(x2, norm2_g[l], mod, router_w[l], router_b[l], w1, b1[l], w2, b2[l], seq, l)
    return x2.reshape(bsz, seq, d)
```

```python
import functools
import math

import jax
import jax.numpy as jnp
from jax import lax
from jax.experimental import pallas as pl
from jax.experimental.pallas import tpu as pltpu

F32 = jnp.float32
BF16 = jnp.bfloat16
I32 = jnp.int32

A_HEADS, A_DH = 4, 64
B_HEADS, B_DH = 4, 128
C_HEADS, C_DH = 4, 128
D_HEADS, D_DH = 4, 128
IDX_HEADS, IDX_DH = 8, 64
N_BRANCH = 4
MOBA_BLOCK = 256
MOBA_TOPK = 3
DSA_TOPK = 256
TOP_K = 4
SWIGLU_ALPHA = 1.702
SWIGLU_LIMIT = 7.0
N_ALIBI = A_HEADS + C_HEADS + D_HEADS
RMS_EPS = 1e-6

LANE = 128
QKV_COLS = 5960
QKV_PAD = 6144
IQ_BLOCK = 40
DK_BLOCK = 44
DV_BLOCK = 45
IK_BLOCK = 46
NEG_BIG = -1e30
INT_MIN = -(2 ** 31)
MOE_SUB = 256
MOE_BLK = 512
VMEM_LIMIT = 56 * 1024 * 1024


def _alibi_slopes():
    return [2.0 ** (-8.0 * (i + 1.0) / N_ALIBI) for i in range(N_ALIBI)]


def _cp(sem, vmem=VMEM_LIMIT):
    return pltpu.CompilerParams(dimension_semantics=sem, vmem_limit_bytes=vmem)


def _dot(a, b):
    return jnp.dot(a, b, preferred_element_type=F32)


def _dot_t(a, b):
    return lax.dot_general(a, b, (((1,), (1,)), ((), ())), preferred_element_type=F32)


def _split(x):
    hi = x.astype(BF16)
    lo = (x - hi.astype(F32)).astype(BF16)
    return hi, lo


def _ada_kernel(ct_ref, w_ref, b_ref, o_ref, *, bsz):
    ct = ct_ref[...]
    cond = ct * jax.nn.sigmoid(ct)
    w = w_ref[0]
    o_ref[...] = jnp.zeros_like(o_ref)
    for b in range(bsz):
        o_ref[0, b:b + 1, :] = jnp.sum(cond[:, b:b + 1] * w, axis=0, keepdims=True) + b_ref[0]


def _ada_mod(c, ada_w, ada_b):
    depth, d, n = ada_w.shape
    bsz = c.shape[0]
    bp = -(-bsz // 8) * 8
    cp = jnp.pad(c, ((0, bp - bsz), (0, 0))).T
    tn = 1024
    out = pl.pallas_call(
        functools.partial(_ada_kernel, bsz=bsz),
        grid=(depth, n // tn),
        in_specs=[pl.BlockSpec((d, bp), lambda l, j: (0, 0)),
                  pl.BlockSpec((1, d, tn), lambda l, j: (l, 0, j)),
                  pl.BlockSpec((1, 1, tn), lambda l, j: (l, 0, j))],
        out_specs=pl.BlockSpec((1, bp, tn), lambda l, j: (l, 0, j)),
        out_shape=jax.ShapeDtypeStruct((depth, bp, n), F32),
        compiler_params=_cp(("parallel", "parallel")),
        name="ada_mod",
    )(cp, ada_w, ada_b.reshape(depth, 1, n))
    return out[:, :bsz].reshape(depth, bsz, 6, d)


def _norm_mod(x, g, mod, sh_idx, sc_idx):
    xx = x * x
    ms = jnp.mean(xx, axis=-1, keepdims=True)
    y = x * lax.rsqrt(ms + RMS_EPS) * g
    return y * (1.0 + mod[sc_idx:sc_idx + 1]) + mod[sh_idx:sh_idx + 1]


def _norm1_kernel(x_ref, g_ref, mod_ref, h_ref):
    h = _norm_mod(x_ref[...], g_ref[...], mod_ref[0], 0, 1)
    h_ref[...] = h.astype(BF16)


def _norm1(x2, g, mod, seq):
    t, d = x2.shape
    tm = min(512, seq)
    per_b = seq // tm
    return pl.pallas_call(
        _norm1_kernel,
        grid=(t // tm,),
        in_specs=[pl.BlockSpec((tm, d), lambda i: (i, 0)),
                  pl.BlockSpec((1, d), lambda i: (0, 0)),
                  pl.BlockSpec((1, 6, d), lambda i: (i // per_b, 0, 0))],
        out_specs=pl.BlockSpec((tm, d), lambda i: (i, 0)),
        out_shape=jax.ShapeDtypeStruct((t, d), BF16),
        compiler_params=_cp(("parallel",)),
        name="norm1",
    )(x2, g.reshape(1, d), mod)


def _router_kernel(x_ref, g_ref, mod_ref, rw_ref, rb_ref,
                   hp_ref, idx_ref, gate_ref, rank_ref, cnt_ref, carry_ref, *, n_exp):
    i = pl.program_id(0)
    tm = x_ref.shape[0]
    half = x_ref.shape[1] // 2

    @pl.when(i == 0)
    def _():
        carry_ref[...] = jnp.zeros_like(carry_ref)

    h = _norm_mod(x_ref[...], g_ref[...], mod_ref[0], 3, 4)
    hb = h.astype(BF16)
    lo = lax.bitcast_convert_type(hb[:, :half].astype(F32), I32)
    hi = lax.bitcast_convert_type(hb[:, half:].astype(F32), I32)
    hp_ref[...] = jnp.bitwise_or(lax.shift_right_logical(lo, 16), jnp.bitwise_and(hi, -65536))

    logits = _dot(hb, rw_ref[...]) + rb_ref[...]
    lane = lax.broadcasted_iota(I32, (tm, LANE), 1)
    lane_f = lane.astype(F32)
    cur = jnp.where(lane < n_exp, logits, -jnp.inf)
    idx_out = jnp.zeros((tm, LANE), F32)
    val_out = jnp.zeros((tm, LANE), F32)
    onehot = jnp.zeros((tm, LANE), F32)
    vals = []
    idxs = []
    for k in range(TOP_K):
        m = jnp.max(cur, axis=-1, keepdims=True)
        am = jnp.min(jnp.where(cur == m, lane_f, float(LANE)), axis=-1, keepdims=True)
        hit = lane_f == am
        cur = jnp.where(hit, -jnp.inf, cur)
        onehot = onehot + jnp.where(hit, 1.0, 0.0)
        idx_out = jnp.where(lane == k, am, idx_out)
        vals.append(m)
        idxs.append(am)
    den = sum(jnp.exp(v - vals[0]) for v in vals)
    for k in range(TOP_K):
        val_out = jnp.where(lane == k, jnp.exp(vals[k] - vals[0]) / den, val_out)
    idx_ref[...] = idx_out.astype(I32)
    gate_ref[...] = val_out

    r = lax.broadcasted_iota(I32, (tm, tm), 0)
    c = lax.broadcasted_iota(I32, (tm, tm), 1)
    tri = jnp.where(c < r, 1.0, 0.0).astype(BF16)
    prefix = _dot(tri, onehot.astype(BF16)) + carry_ref[...]
    rank_out = jnp.zeros((tm, LANE), F32)
    for k in range(TOP_K):
        rk = jnp.sum(jnp.where(lane_f == idxs[k], prefix, 0.0), axis=-1, keepdims=True)
        rank_out = jnp.where(lane == k, rk, rank_out)
    rank_ref[...] = rank_out.astype(I32)
    tot = carry_ref[...] + jnp.sum(onehot, axis=0, keepdims=True)
    carry_ref[...] = tot
    cnt_ref[...] = jnp.broadcast_to(tot, cnt_ref.shape).astype(I32)


def _router(x2, g, mod, rw, rb, seq, n_exp):
    t, d = x2.shape
    tm = min(256, seq)
    per_b = seq // tm
    outs = pl.pallas_call(
        functools.partial(_router_kernel, n_exp=n_exp),
        grid=(t // tm,),
        in_specs=[pl.BlockSpec((tm, d), lambda i: (i, 0)),
                  pl.BlockSpec((1, d), lambda i: (0, 0)),
                  pl.BlockSpec((1, 6, d), lambda i: (i // per_b, 0, 0)),
                  pl.BlockSpec((d, LANE), lambda i: (0, 0)),
                  pl.BlockSpec((1, LANE), lambda i: (0, 0))],
        out_specs=[pl.BlockSpec((tm, d // 2), lambda i: (i, 0)),
                   pl.BlockSpec((tm, LANE), lambda i: (i, 0)),
                   pl.BlockSpec((tm, LANE), lambda i: (i, 0)),
                   pl.BlockSpec((tm, LANE), lambda i: (i, 0)),
                   pl.BlockSpec((8, LANE), lambda i: (0, 0))],
        out_shape=[jax.ShapeDtypeStruct((t, d // 2), I32),
                   jax.ShapeDtypeStruct((t, LANE), I32),
                   jax.ShapeDtypeStruct((t, LANE), F32),
                   jax.ShapeDtypeStruct((t, LANE), I32),
                   jax.ShapeDtypeStruct((8, LANE), I32)],
        scratch_shapes=[pltpu.VMEM((1, LANE), F32)],
        compiler_params=_cp(("arbitrary",)),
        name="router",
    )(x2, g.reshape(1, d), mod, rw, rb)
    return outs


def _mm_kernel(a_ref, b_ref, o_ref):
    o_ref[...] = _dot(a_ref[...], b_ref[...]).astype(o_ref.dtype)


def _matmul(a, b, out_dtype, tm, tn, name):
    m, k = a.shape
    n = b.shape[1]
    tm = min(tm, m)
    return pl.pallas_call(
        _mm_kernel,
        grid=(m // tm, n // tn),
        in_specs=[pl.BlockSpec((tm, k), lambda i, j: (i, 0)),
                  pl.BlockSpec((k, tn), lambda i, j: (0, j))],
        out_specs=pl.BlockSpec((tm, tn), lambda i, j: (i, j)),
        out_shape=jax.ShapeDtypeStruct((m, n), out_dtype),
        compiler_params=_cp(("parallel", "parallel")),
        name=name,
    )(a, b)


def _out_proj_kernel(y_ref, w_ref, x_ref, mod_ref, o_ref):
    g = mod_ref[0][2:3]
    o_ref[...] = x_ref[...] + g * _dot(y_ref[...], w_ref[...])


def _out_proj(y, w, x2, mod, seq):
    m, k = y.shape
    n = w.shape[1]
    tm = min(1024, seq)
    tn = 512
    per_b = seq // tm
    return pl.pallas_call(
        _out_proj_kernel,
        grid=(m // tm, n // tn),
        in_specs=[pl.BlockSpec((tm, k), lambda i, j: (i, 0)),
                  pl.BlockSpec((k, tn), lambda i, j: (0, j)),
                  pl.BlockSpec((tm, tn), lambda i, j: (i, j)),
                  pl.BlockSpec((1, 6, tn), lambda i, j: (i // per_b, 0, j))],
        out_specs=pl.BlockSpec((tm, tn), lambda i, j: (i, j)),
        out_shape=jax.ShapeDtypeStruct((m, n), F32),
        compiler_params=_cp(("parallel", "parallel")),
        name="out_proj",
    )(y, w, x2, mod)


QKV_TILE = 4 * LANE
HALF_NORM_TILES = (0, 1)
FULL_NORM_TILES = (6, 7, 9)
LAST_TILE = 11


def _qkv_kernel(a_ref, b_ref, g_ref, q_ref, w_ref):
    j = pl.program_id(1)
    tm = a_ref.shape[0]
    y = _dot(a_ref[...], b_ref[...])
    gain = g_ref[pl.ds(j, 1), :]
    lane = lax.broadcasted_iota(I32, (tm, LANE), 1)
    low = lane < (LANE // 2)

    def blk(c):
        return y[:, c * LANE:(c + 1) * LANE]

    def put(c, v):
        q_ref[:, c * LANE:(c + 1) * LANE] = v.astype(BF16)

    def norm_full(v):
        ms = jnp.mean(v * v, axis=-1, keepdims=True)
        return v * lax.rsqrt(ms + RMS_EPS) * gain

    def norm_half(v):
        vv = v * v
        s_lo = jnp.sum(jnp.where(low, vv, 0.0), axis=-1, keepdims=True)
        s_hi = jnp.sum(jnp.where(low, 0.0, vv), axis=-1, keepdims=True)
        ms = jnp.where(low, s_lo, s_hi) * (2.0 / LANE)
        return v * lax.rsqrt(ms + RMS_EPS) * gain

    is_half = functools.reduce(jnp.logical_or, [j == n for n in HALF_NORM_TILES])
    is_full = functools.reduce(jnp.logical_or, [j == n for n in FULL_NORM_TILES])
    is_last = j == LAST_TILE

    @pl.when(is_half)
    def _():
        for c in range(QKV_TILE // LANE):
            put(c, norm_half(blk(c)))

    @pl.when(is_full)
    def _():
        for c in range(QKV_TILE // LANE):
            put(c, norm_full(blk(c)))

    @pl.when(is_last)
    def _():
        put(0, norm_full(blk(0)))
        put(1, blk(1))
        ik = blk(2)
        w_ref[...] = ik
        put(2, jnp.where(low, ik, pltpu.roll(ik, LANE // 2, 1)))
        put(3, blk(3))

    @pl.when(jnp.logical_not(is_half | is_full | is_last))
    def _():
        for c in range(QKV_TILE // LANE):
            put(c, blk(c) * gain)


def _qkv_proj(h, w_qkv, gains):
    t, k = h.shape
    tm = min(1024, t)
    assert DK_BLOCK * LANE == LAST_TILE * QKV_TILE and IK_BLOCK == DK_BLOCK + 2
    return pl.pallas_call(
        _qkv_kernel,
        grid=(t // tm, QKV_PAD // QKV_TILE),
        in_specs=[pl.BlockSpec((tm, k), lambda i, j: (i, 0)),
                  pl.BlockSpec((k, QKV_TILE), lambda i, j: (0, j)),
                  pl.BlockSpec(gains.shape, lambda i, j: (0, 0))],
        out_specs=[pl.BlockSpec((tm, QKV_TILE), lambda i, j: (i, j)),
                   pl.BlockSpec((tm, LANE), lambda i, j: (i, 0))],
        out_shape=[jax.ShapeDtypeStruct((t, QKV_PAD), BF16),
                   jax.ShapeDtypeStruct((t, LANE), F32)],
        compiler_params=_cp(("parallel", "arbitrary")),
        name="in_proj_qkv",
    )(h, w_qkv, gains)


ATT_TILE = 256


def _dot_tn(a, b):
    return lax.dot_general(a, b, (((0,), (0,)), ((), ())), preferred_element_type=F32)


Q_STRIP = LANE
LOG2E = math.log2(math.e)


def _flash_tile(k, v, q_fn, bias_fn, keep_fn, m_ref, l_ref, acc_ref, h, first):
    for n in range(m_ref.shape[2] // Q_STRIP):
        ql = slice(n * Q_STRIP, (n + 1) * Q_STRIP)
        s = _dot_t(k, q_fn(n)) + bias_fn(n)
        keep = None if keep_fn is None else keep_fn(n)
        if keep is not None:
            s = jnp.where(keep, s, NEG_BIG)
        m_new = jnp.max(s, axis=0, keepdims=True)
        if not first:
            m_old = m_ref[h, :, ql]
            m_new = jnp.maximum(m_old, m_new)
        p = jnp.exp2(s - m_new)
        if keep is not None:
            p = jnp.where(keep, p, 0.0)
        p_sum = jnp.sum(p, axis=0, keepdims=True)
        pv = _dot_tn(v, p.astype(BF16))
        if first:
            l_ref[h, :, ql] = p_sum
            acc_ref[h, :, ql] = pv
        else:
            alpha = jnp.exp2(m_old - m_new)
            l_ref[h, :, ql] = alpha * l_ref[h, :, ql] + p_sum
            acc_ref[h, :, ql] = alpha * acc_ref[h, :, ql] + pv
        m_ref[h, :, ql] = m_new


def _head_cols(h):
    return slice(h * LANE, (h + 1) * LANE)


def _strip_rows(ref, h, n):
    return ref[h, n * Q_STRIP:(n + 1) * Q_STRIP, :]


def _strip_cols(x, n):
    return x[:, n * Q_STRIP:(n + 1) * Q_STRIP]


def _diff_kernel(sc_ref, q_ref, k_ref, v_ref, g_ref, o_ref, q12_ref, m_ref, l_ref, acc_ref, *, t, slopes):
    i = pl.program_id(1)
    lam = sc_ref[0]
    post = sc_ref[1]
    lane = lax.broadcasted_iota(I32, (t, LANE), 1)
    low = lane < (LANE // 2)
    causal = lax.broadcasted_iota(I32, (t, t), 0) <= lax.broadcasted_iota(I32, (t, t), 1)
    kpos = lax.broadcasted_iota(I32, (t, 1), 0).astype(F32)
    off_d = pl.multiple_of(i * t, t)
    for h in range(A_HEADS):
        cs = _head_cols(h)
        q = q_ref[:, cs]
        zero = jnp.zeros_like(q)
        q12_ref[h] = jnp.concatenate([jnp.where(low, q, zero), jnp.where(low, zero, q)], axis=0)
        bias = jnp.where(causal, slopes[h] * kpos, NEG_BIG)
        _flash_tile(k_ref[pl.ds(off_d, t), cs], v_ref[pl.ds(off_d, t), cs],
                    functools.partial(_strip_rows, q12_ref, h),
                    lambda n, bias=bias: _strip_cols(bias, n % (t // Q_STRIP)), None,
                    m_ref, l_ref, acc_ref, h, first=True)

    def body(kt, _):
        off = pl.multiple_of(kt * t, t)
        shift = ((kt - i) * t).astype(F32)
        for h in range(A_HEADS):
            cs = _head_cols(h)
            bias = slopes[h] * (kpos + shift)
            _flash_tile(k_ref[pl.ds(off, t), cs], v_ref[pl.ds(off, t), cs],
                        functools.partial(_strip_rows, q12_ref, h), lambda n, bias=bias: bias, None,
                        m_ref, l_ref, acc_ref, h, first=False)
        return 0

    lax.fori_loop(0, i, body, 0)
    for h in range(A_HEADS):
        o = acc_ref[h] / l_ref[h]
        o = o[:, :t] - lam * o[:, t:]
        ms = jnp.mean(o * o, axis=0, keepdims=True)
        o = o * lax.rsqrt(ms + RMS_EPS) * g_ref[...] * post
        o_ref[:, _head_cols(h)] = o.T.astype(BF16)


def _diff_attention(qkv, scal, subln_g, bsz, seq, slopes):
    t = ATT_TILE
    nq = seq // t
    hw = A_HEADS * LANE
    return pl.pallas_call(
        functools.partial(_diff_kernel, t=t, slopes=slopes),
        grid=(bsz, nq),
        in_specs=[pl.BlockSpec(memory_space=pltpu.SMEM),
                  pl.BlockSpec((t, hw), lambda b, i: (b * nq + i, 0)),
                  pl.BlockSpec((seq, hw), lambda b, i: (b, 1)),
                  pl.BlockSpec((seq, hw), lambda b, i: (b, 2)),
                  pl.BlockSpec((LANE, 1), lambda b, i: (0, 0))],
        out_specs=pl.BlockSpec((t, hw), lambda b, i: (b * nq + i, 0)),
        out_shape=jax.ShapeDtypeStruct((bsz * seq, hw), BF16),
        scratch_shapes=[pltpu.VMEM((A_HEADS, 2 * t, LANE), BF16),
                        pltpu.VMEM((A_HEADS, 1, 2 * t), F32),
                        pltpu.VMEM((A_HEADS, 1, 2 * t), F32),
                        pltpu.VMEM((A_HEADS, LANE, 2 * t), F32)],
        compiler_params=_cp(("parallel", "parallel")),
        name="diff_attn",
    )(scal, qkv, qkv, qkv, subln_g.reshape(LANE, 1))


def _stick_kernel(q_ref, k_ref, v_ref, o_ref, tail_ref, acc_ref, *, t):
    i = pl.program_id(1)
    r = lax.broadcasted_iota(I32, (t, t), 0)
    c = lax.broadcasted_iota(I32, (t, t), 1)
    strict = r < c
    later = jnp.where(c > r, 1.0, 0.0).astype(BF16)

    def tile(off, h, masked):
        cs = _head_cols(h)
        z = _dot_t(k_ref[pl.ds(off, t), cs], q_ref[:, cs])
        ls = jnp.minimum(z, 0.0) - jnp.log1p(jnp.exp(-jnp.abs(z)))
        l1 = ls - z
        if masked:
            l1 = jnp.where(strict, l1, 0.0)
        hi, lo = _split(l1)
        after = _dot(later, hi) + _dot(later, lo) + tail_ref[h]
        w = jnp.exp(ls + after)
        if masked:
            w = jnp.where(strict, w, 0.0)
        acc_ref[h] = acc_ref[h] + _dot_tn(v_ref[pl.ds(off, t), cs], w.astype(BF16))
        tail_ref[h] = tail_ref[h] + jnp.sum(l1, axis=0, keepdims=True)

    tail_ref[...] = jnp.zeros_like(tail_ref)
    acc_ref[...] = jnp.zeros_like(acc_ref)
    off_d = pl.multiple_of(i * t, t)
    for h in range(B_HEADS):
        tile(off_d, h, True)

    def body(j, _):
        off = pl.multiple_of((i - 1 - j) * t, t)
        for h in range(B_HEADS):
            tile(off, h, False)
        return 0

    lax.fori_loop(0, i, body, 0)
    for h in range(B_HEADS):
        o_ref[:, _head_cols(h)] = acc_ref[h].T.astype(BF16)


def _stick_attention(qkv, bsz, seq):
    t = ATT_TILE
    nq = seq // t
    hw = B_HEADS * LANE
    return pl.pallas_call(
        functools.partial(_stick_kernel, t=t),
        grid=(bsz, nq),
        in_specs=[pl.BlockSpec((t, hw), lambda b, i: (b * nq + i, 3)),
                  pl.BlockSpec((seq, hw), lambda b, i: (b, 4)),
                  pl.BlockSpec((seq, hw), lambda b, i: (b, 5))],
        out_specs=pl.BlockSpec((t, hw), lambda b, i: (b * nq + i, 0)),
        out_shape=jax.ShapeDtypeStruct((bsz * seq, hw), BF16),
        scratch_shapes=[pltpu.VMEM((B_HEADS, 1, t), F32),
                        pltpu.VMEM((B_HEADS, LANE, t), F32)],
        compiler_params=_cp(("parallel", "parallel")),
        name="stick_attn",
    )(qkv, qkv, qkv)


def _moba_kernel(q_ref, k_ref, v_ref, o_ref, sel_ref, m_ref, l_ref, acc_ref, *, t, n_kb, n_sel, slopes):
    i = pl.program_id(1)
    seq = k_ref.shape[0]
    blk = lax.broadcasted_iota(I32, (LANE, t), 0)
    causal = lax.broadcasted_iota(I32, (t, t), 0) <= lax.broadcasted_iota(I32, (t, t), 1)
    kpos = lax.broadcasted_iota(I32, (t, 1), 0).astype(F32)
    lo_edge = lax.broadcasted_iota(I32, (LANE, seq), 0) * t
    bc = lax.broadcasted_iota(I32, (LANE, seq), 1)
    avg = jnp.where((bc >= lo_edge) & (bc < lo_edge + t), 1.0 / t, 0.0).astype(BF16)
    off_d = pl.multiple_of(i * t, t)
    for h in range(C_HEADS):
        cs = _head_cols(h)
        q = q_ref[:, cs]
        k_mean = _dot(avg, k_ref[:, cs]).astype(BF16)
        gate = jnp.where(blk < i, _dot_t(k_mean, q), -jnp.inf)
        rank = jnp.zeros((LANE, t), F32)
        for mth in range(n_kb):
            gm = gate[mth:mth + 1, :]
            beats = (gm > gate) | ((gm == gate) & (blk > mth))
            rank = rank + jnp.where(beats, 1.0, 0.0)
        sel_ref[h] = jnp.where((rank < n_sel) & (blk < i), 1.0, 0.0)
        bias = jnp.where(causal, slopes[h] * kpos, NEG_BIG)
        _flash_tile(k_ref[pl.ds(off_d, t), cs], v_ref[pl.ds(off_d, t), cs],
                    lambda n, cs=cs: q_ref[n * Q_STRIP:(n + 1) * Q_STRIP, cs],
                    lambda n, bias=bias: _strip_cols(bias, n), None,
                    m_ref, l_ref, acc_ref, h, first=True)

    def body(kt, _):
        off = pl.multiple_of(kt * t, t)
        shift = ((kt - i) * t).astype(F32)
        for h in range(C_HEADS):
            cs = _head_cols(h)
            bias = slopes[h] * (kpos + shift)
            flag = sel_ref[h, pl.ds(kt, 1), :]
            _flash_tile(k_ref[pl.ds(off, t), cs], v_ref[pl.ds(off, t), cs],
                        lambda n, cs=cs: q_ref[n * Q_STRIP:(n + 1) * Q_STRIP, cs],
                        lambda n, bias=bias: bias,
                        lambda n, flag=flag: jnp.broadcast_to(_strip_cols(flag, n), (t, Q_STRIP)) > 0.0,
                        m_ref, l_ref, acc_ref, h, first=False)
        return 0

    lax.fori_loop(0, i, body, 0)
    for h in range(C_HEADS):
        o_ref[:, _head_cols(h)] = (acc_ref[h] / l_ref[h]).T.astype(BF16)


def _moba_attention(qkv, bsz, seq, slopes):
    t = MOBA_BLOCK
    assert seq % t == 0 and t == ATT_TILE
    n_kb = seq // t
    assert n_kb <= LANE
    n_sel = min(MOBA_TOPK, n_kb - 1)
    hw = C_HEADS * LANE
    return pl.pallas_call(
        functools.partial(_moba_kernel, t=t, n_kb=n_kb, n_sel=n_sel, slopes=slopes),
        grid=(bsz, n_kb),
        in_specs=[pl.BlockSpec((t, hw), lambda b, i: (b * n_kb + i, 6)),
                  pl.BlockSpec((seq, hw), lambda b, i: (b, 7)),
                  pl.BlockSpec((seq, hw), lambda b, i: (b, 8))],
        out_specs=pl.BlockSpec((t, hw), lambda b, i: (b * n_kb + i, 0)),
        out_shape=jax.ShapeDtypeStruct((bsz * seq, hw), BF16),
        scratch_shapes=[pltpu.VMEM((C_HEADS, LANE, t), F32),
                        pltpu.VMEM((C_HEADS, 1, t), F32),
                        pltpu.VMEM((C_HEADS, 1, t), F32),
                        pltpu.VMEM((C_HEADS, LANE, t), F32)],
        compiler_params=_cp(("parallel", "parallel")),
        name="moba_attn",
    )(qkv, qkv, qkv)


def _dsa_kernel(q_ref, k_ref, v_ref, iq_ref, ki_ref, w_ref, o_ref,
                keys_ref, iqm_ref, cut_ref, m_ref, l_ref, acc_ref, *, t, n_keep, idx_bits, slopes):
    i = pl.program_id(1)
    n_kt = i + 1
    lane = lax.broadcasted_iota(I32, (t, LANE), 1)
    low = lane < (LANE // 2)
    kidx = lax.broadcasted_iota(I32, (t, t), 0)
    qpos = i * t + lax.broadcasted_iota(I32, (t, t), 1)
    kpos = lax.broadcasted_iota(I32, (t, 1), 0).astype(F32)
    w_scale = (IDX_DH ** -0.5) * (IDX_HEADS ** -0.5)
    w_t = w_ref[...].T * w_scale
    for p in range(IDX_HEADS // 2):
        qp = iq_ref[:, p * LANE:(p + 1) * LANE]
        zero = jnp.zeros_like(qp)
        iqm_ref[2 * p] = jnp.where(low, qp, zero)
        iqm_ref[2 * p + 1] = jnp.where(low, zero, qp)

    def score_body(kt, _):
        off = pl.multiple_of(kt * t, t)
        kk = ki_ref[pl.ds(off, t), :]
        sc = jnp.zeros((t, t), F32)
        for j in range(IDX_HEADS):
            wj = w_t[LANE // 2 + j:LANE // 2 + j + 1, :]
            sc = sc + jnp.maximum(_dot_t(kk, iqm_ref[j]), 0.0) * wj
        sc = jnp.where(sc == 0.0, 0.0, sc)
        bits = lax.bitcast_convert_type(sc, I32)
        key = jnp.bitwise_xor(bits, jnp.bitwise_and(lax.shift_right_arithmetic(bits, 31), 0x7FFFFFFF))
        keys_ref[pl.ds(off, t), :] = jnp.where(off + kidx <= qpos, key, INT_MIN)
        return 0

    lax.fori_loop(0, n_kt, score_body, 0)

    def count(pred):
        def body(kt, acc):
            off = pl.multiple_of(kt * t, t)
            hit = jnp.where(pred(keys_ref[pl.ds(off, t), :], off + kidx), 1.0, 0.0)
            return acc + jnp.sum(hit, axis=0, keepdims=True)
        return lax.fori_loop(0, n_kt, body, jnp.zeros((1, t), F32))

    keep_f = float(n_keep)
    zero_t = jnp.zeros((1, t), I32)
    c0 = count(lambda kc, col: kc >= zero_t)
    thr = jnp.where(c0 >= keep_f, 0, INT_MIN).astype(I32)
    n_ge = jnp.where(c0 >= keep_f, c0, (n_kt * t).astype(F32))

    def bit_body(b, state):
        thr, n_ge = state
        cand = jnp.bitwise_or(thr, lax.shift_left(jnp.int32(1), 30 - b))
        cnt = count(lambda kc, col: kc >= cand)
        take = cnt >= keep_f
        return jnp.where(take, cand, thr), jnp.where(take, cnt, n_ge)

    thr, n_ge = lax.fori_loop(0, 31, bit_body, (thr, n_ge))

    need = (n_ge > keep_f) & (thr > INT_MIN)
    big = jnp.int32(2 ** 30)
    cut_ref[...] = jnp.full((1, t), big, I32)

    @pl.when(jnp.max(jnp.where(need, 1.0, 0.0)) > 0.0)
    def _():
        quota = keep_f - count(lambda kc, col: kc > thr)

        def tie_body(b, cut):
            cand = jnp.bitwise_or(cut, lax.shift_left(jnp.int32(1), idx_bits - 1 - b))
            cnt = count(lambda kc, col: (kc == thr) & (col < cand))
            return jnp.where(cnt <= quota, cand, cut)
        cut = lax.fori_loop(0, idx_bits, tie_body, jnp.zeros((1, t), I32))
        cut_ref[...] = jnp.where(need, cut, big)

    cut = cut_ref[...]

    def flag_body(kt, _):
        off = pl.multiple_of(kt * t, t)
        kc = keys_ref[pl.ds(off, t), :]
        col = off + kidx
        sel = ((kc > thr) | ((kc == thr) & (col < cut))) & (col <= qpos)
        keys_ref[pl.ds(off, t), :] = jnp.where(sel, 1, 0).astype(I32)
        return 0

    lax.fori_loop(0, n_kt, flag_body, 0)

    m_ref[...] = jnp.full(m_ref.shape, NEG_BIG, F32)
    l_ref[...] = jnp.zeros_like(l_ref)
    acc_ref[...] = jnp.zeros_like(acc_ref)

    def body(kt, _):
        off = pl.multiple_of(kt * t, t)
        shift = ((kt - i) * t).astype(F32)
        k = k_ref[pl.ds(off, t), :]
        v = v_ref[pl.ds(off, t), :]
        for h in range(D_HEADS):
            cs = _head_cols(h)
            bias = slopes[h] * (kpos + shift)
            _flash_tile(k, v,
                        lambda n, cs=cs: q_ref[n * Q_STRIP:(n + 1) * Q_STRIP, cs],
                        lambda n, bias=bias: bias,
                        lambda n: keys_ref[pl.ds(off, t), n * Q_STRIP:(n + 1) * Q_STRIP] > 0,
                        m_ref, l_ref, acc_ref, h, first=False)
        return 0

    lax.fori_loop(0, n_kt, body, 0)
    for h in range(D_HEADS):
        o_ref[:, _head_cols(h)] = (acc_ref[h] / l_ref[h]).T.astype(BF16)


def _dsa_attention(qkv, wraw, bsz, seq, slopes):
    t = ATT_TILE
    nq = seq // t
    n_keep = min(DSA_TOPK, seq // 4)
    idx_bits = int(math.ceil(math.log2(seq))) + 1
    hw = D_HEADS * LANE
    iw = IDX_HEADS * IDX_DH
    return pl.pallas_call(
        functools.partial(_dsa_kernel, t=t, n_keep=n_keep, idx_bits=idx_bits, slopes=slopes),
        grid=(bsz, nq),
        in_specs=[pl.BlockSpec((t, hw), lambda b, i: (b * nq + i, 9)),
                  pl.BlockSpec((seq, LANE), lambda b, i: (b, DK_BLOCK)),
                  pl.BlockSpec((seq, LANE), lambda b, i: (b, DV_BLOCK)),
                  pl.BlockSpec((t, iw), lambda b, i: (b * nq + i, IQ_BLOCK * LANE // iw)),
                  pl.BlockSpec((seq, LANE), lambda b, i: (b, IK_BLOCK)),
                  pl.BlockSpec((t, LANE), lambda b, i: (b * nq + i, 0))],
        out_specs=pl.BlockSpec((t, hw), lambda b, i: (b * nq + i, 0)),
        out_shape=jax.ShapeDtypeStruct((bsz * seq, hw), BF16),
        scratch_shapes=[pltpu.VMEM((seq, t), I32),
                        pltpu.VMEM((IDX_HEADS, t, LANE), BF16),
                        pltpu.VMEM((1, t), I32),
                        pltpu.VMEM((D_HEADS, 1, t), F32),
                        pltpu.VMEM((D_HEADS, 1, t), F32),
                        pltpu.VMEM((D_HEADS, LANE, t), F32)],
        compiler_params=_cp(("parallel", "parallel")),
        name="dsa_attn",
    )(qkv, qkv, qkv, qkv, qkv, wraw)


def _merge_kernel(a_ref, b_ref, c_ref, d_ref, wb_ref, g0, g1, g2, g3, gb_ref, y_ref):
    acc = None
    for n, (br, gl) in enumerate(zip((a_ref, b_ref, c_ref, d_ref), (g0, g1, g2, g3))):
        up = _dot(br[...], wb_ref[n])
        gate = jax.nn.sigmoid(gl[...].astype(F32) + gb_ref[n])
        acc = gate * up if acc is None else acc + gate * up
    y_ref[...] = acc.astype(BF16)


def _merge(branches, wb, gl, gate_b):
    t, bw = branches[0].shape
    d = wb.shape[2]
    tm = min(1024, t)
    tn = 512
    nj = d // tn
    br_spec = pl.BlockSpec((tm, bw), lambda j, i: (i, 0))
    gl_specs = [pl.BlockSpec((tm, tn), functools.partial(lambda j, i, n: (i, n * nj + j), n=n))
                for n in range(N_BRANCH)]
    return pl.pallas_call(
        _merge_kernel,
        grid=(nj, t // tm),
        in_specs=[br_spec] * 4 + [pl.BlockSpec((N_BRANCH, bw, tn), lambda j, i: (0, 0, j))] + gl_specs
                 + [pl.BlockSpec((N_BRANCH, 1, tn), lambda j, i: (0, 0, j))],
        out_specs=pl.BlockSpec((tm, tn), lambda j, i: (i, j)),
        out_shape=jax.ShapeDtypeStruct((t, d), BF16),
        compiler_params=_cp(("parallel", "parallel")),
        name="merge",
    )(*branches, wb, gl, gl, gl, gl, gate_b.reshape(N_BRANCH, 1, d))


def _unpack_rows(xp):
    lo = lax.bitcast_convert_type(lax.shift_left(xp, 16), F32).astype(BF16)
    hi = lax.bitcast_convert_type(jnp.bitwise_and(xp, -65536), F32).astype(BF16)
    return lo, hi


def _gather_rows_kernel(tok_ref, live_ref, src_ref, out_ref, buf, sems, *, rows):
    g = pl.program_id(0)
    n_sub = rows // MOE_SUB
    half = buf.shape[1]
    for c in range(n_sub):
        live = live_ref[g * n_sub + c] > 0
        lo = c * MOE_SUB

        @pl.when(live)
        def _(lo=lo, c=c):
            def issue(r, _):
                pltpu.make_async_copy(src_ref.at[pl.ds(tok_ref[g * rows + lo + r], 1)],
                                      buf.at[pl.ds(lo + r, 1)], sems.at[c]).start()
                return 0
            lax.fori_loop(0, MOE_SUB, issue, 0, unroll=8)

        @pl.when(jnp.logical_not(live))
        def _(lo=lo):
            out_ref[lo:lo + MOE_SUB, :] = jnp.zeros((MOE_SUB, out_ref.shape[1]), out_ref.dtype)

    for c in range(n_sub):
        lo = c * MOE_SUB

        @pl.when(live_ref[g * n_sub + c] > 0)
        def _(lo=lo, c=c):
            pltpu.make_async_copy(src_ref.at[pl.ds(0, MOE_SUB)], buf.at[pl.ds(lo, MOE_SUB)], sems.at[c]).wait()
            x_lo, x_hi = _unpack_rows(buf[lo:lo + MOE_SUB, :])
            out_ref[lo:lo + MOE_SUB, :half] = x_lo
            out_ref[lo:lo + MOE_SUB, half:] = x_hi


def _gather_rows(src, row_tok, sub_live):
    n_rows = row_tok.shape[0]
    half = src.shape[1]
    rows = max(r for r in (4 * MOE_SUB, 2 * MOE_SUB) if n_rows % r == 0)
    return pl.pallas_call(
        functools.partial(_gather_rows_kernel, rows=rows),
        grid_spec=pltpu.PrefetchScalarGridSpec(
            num_scalar_prefetch=2,
            grid=(n_rows // rows,),
            in_specs=[pl.BlockSpec(memory_space=pl.ANY)],
            out_specs=pl.BlockSpec((rows, 2 * half), lambda i, tok, lv: (i, 0)),
            scratch_shapes=[pltpu.VMEM((rows, half), src.dtype),
                            pltpu.SemaphoreType.DMA((rows // MOE_SUB,))]),
        out_shape=jax.ShapeDtypeStruct((n_rows, 2 * half), BF16),
        compiler_params=_cp(("arbitrary",)),
        name="moe_gather",
    )(row_tok, sub_live, src)


def _stream_expert_weights(copies, cast, be_ref, bf_ref, br_ref, bn_ref, misc_ref):
    j = pl.program_id(0)
    i = pl.program_id(1)
    n_chunks = pl.num_programs(0)
    n_runs = misc_ref[1]

    @pl.when((j == 0) & (i == 0))
    def _():
        for cp in copies(be_ref[0], 0, 0):
            cp.start()

    @pl.when(bf_ref[i] > 0)
    def _():
        run = br_ref[i]
        slot = (j * n_runs + run) % 2
        for cp in copies(be_ref[i], j, slot):
            cp.wait()
        cast(slot)
        j_next = jnp.where(run == n_runs - 1, j + 1, j)

        @pl.when(j_next < n_chunks)
        def _():
            for cp in copies(bn_ref[i], j_next, 1 - slot):
                cp.start()


def _live_rows_dispatch(n_live, out_ref, compute):
    for n in range(1, MOE_BLK // MOE_SUB + 1):
        @pl.when(n_live == n)
        def _(n=n):
            rows = n * MOE_SUB
            compute(rows)
            if rows < MOE_BLK:
                out_ref[rows:, :] = jnp.zeros((MOE_BLK - rows, out_ref.shape[1]), out_ref.dtype)

    @pl.when(n_live == 0)
    def _():
        out_ref[...] = jnp.zeros_like(out_ref)


def _expert_up_kernel(be_ref, bl_ref, bf_ref, br_ref, bn_ref, misc_ref, xp_ref, w_hbm, bg_ref, bb_ref, act_ref,
                      wbuf, wg_s, wl_s, sems, *, layer, d_ff):
    i = pl.program_id(1)
    n_live = bl_ref[i]
    tf = wg_s.shape[1]

    def copies(e, chunk, slot):
        col = pl.multiple_of(chunk * tf, tf)
        return [pltpu.make_async_copy(w_hbm.at[layer, e, :, pl.ds(part * d_ff + col, tf)],
                                      wbuf.at[slot, part], sems.at[slot, part]) for part in range(2)]

    def cast(slot):
        wg_s[...] = wbuf[slot, 0].astype(BF16)
        wl_s[...] = wbuf[slot, 1].astype(BF16)

    _stream_expert_weights(copies, cast, be_ref, bf_ref, br_ref, bn_ref, misc_ref)

    def compute(rows):
        x = xp_ref[0:rows, :]
        glu = _dot(x, wg_s[...]) + bg_ref[0]
        lin = _dot(x, wl_s[...]) + bb_ref[0]
        glu = jnp.minimum(glu, SWIGLU_LIMIT)
        lin = jnp.clip(lin, -SWIGLU_LIMIT, SWIGLU_LIMIT)
        act = glu * jax.nn.sigmoid(SWIGLU_ALPHA * glu) * (lin + 1.0)
        act_ref[0:rows, :] = act.astype(BF16)

    _live_rows_dispatch(n_live, act_ref, compute)


def _expert_up(sched, xs, w1, b1, d_ff, l):
    n_rows, d = xs.shape
    mb = MOE_BLK
    tf = 512
    nf = d_ff // tf
    n_blocks = n_rows // mb

    def rows(j, i, be, bl, bf, br, bn, misc):
        return (jnp.minimum(i, misc[0] - 1), 0)

    return pl.pallas_call(
        functools.partial(_expert_up_kernel, layer=l, d_ff=d_ff),
        grid_spec=pltpu.PrefetchScalarGridSpec(
            num_scalar_prefetch=len(sched),
            grid=(nf, n_blocks),
            in_specs=[pl.BlockSpec((mb, d), rows),
                      pl.BlockSpec(memory_space=pl.ANY),
                      pl.BlockSpec((1, 1, tf), lambda j, i, be, *_: (be[i], 0, j)),
                      pl.BlockSpec((1, 1, tf), lambda j, i, be, *_: (be[i], 0, nf + j))],
            out_specs=pl.BlockSpec((mb, tf), lambda j, i, *_: (i, j)),
            scratch_shapes=[pltpu.VMEM((2, 2, d, tf), F32),
                            pltpu.VMEM((d, tf), BF16), pltpu.VMEM((d, tf), BF16),
                            pltpu.SemaphoreType.DMA((2, 2))]),
        out_shape=jax.ShapeDtypeStruct((n_rows, d_ff), BF16),
        compiler_params=_cp(("arbitrary", "arbitrary")),
        name="expert_up",
    )(*sched, xs, w1, b1, b1)


def _expert_down_kernel(be_ref, bl_ref, bf_ref, br_ref, bn_ref, misc_ref, a_ref, w_hbm, b_ref, y_ref,
                        wbuf, w_s, sems, *, layer):
    i = pl.program_id(1)
    n_live = bl_ref[i]
    tn = w_s.shape[1]

    def copies(e, chunk, slot):
        col = pl.multiple_of(chunk * tn, tn)
        return [pltpu.make_async_copy(w_hbm.at[layer, e, :, pl.ds(col, tn)], wbuf.at[slot], sems.at[slot])]

    def cast(slot):
        w_s[...] = wbuf[slot].astype(BF16)

    _stream_expert_weights(copies, cast, be_ref, bf_ref, br_ref, bn_ref, misc_ref)

    def compute(rows):
        y_ref[0:rows, :] = _dot(a_ref[0:rows, :], w_s[...]) + b_ref[0]

    _live_rows_dispatch(n_live, y_ref, compute)


def _expert_down(sched, act, w2, b2, l):
    n_rows, d_ff = act.shape
    d = w2.shape[3]
    mb = MOE_BLK
    tn = 1024
    nn = d // tn
    n_blocks = n_rows // mb

    def rows(j, i, be, bl, bf, br, bn, misc):
        return (jnp.minimum(i, misc[0] - 1), 0)

    return pl.pallas_call(
        functools.partial(_expert_down_kernel, layer=l),
        grid_spec=pltpu.PrefetchScalarGridSpec(
            num_scalar_prefetch=len(sched),
            grid=(nn, n_blocks),
            in_specs=[pl.BlockSpec((mb, d_ff), rows),
                      pl.BlockSpec(memory_space=pl.ANY),
                      pl.BlockSpec((1, 1, tn), lambda j, i, be, *_: (be[i], 0, j))],
            out_specs=pl.BlockSpec((mb, tn), lambda j, i, *_: (i, j)),
            scratch_shapes=[pltpu.VMEM((2, d_ff, tn), F32),
                            pltpu.VMEM((d_ff, tn), BF16),
                            pltpu.SemaphoreType.DMA((2,))]),
        out_shape=jax.ShapeDtypeStruct((n_rows, d), F32),
        compiler_params=_cp(("arbitrary", "arbitrary")),
        name="expert_down",
    )(*sched, act, w2, b2)


def _combine_kernel(dest_ref, y_ref, x_ref, gate_ref, mod_ref, o_ref, buf, sem, *, tm):
    base = pl.program_id(0) * tm * TOP_K

    def issue(r, _):
        for k in range(TOP_K):
            pltpu.make_async_copy(y_ref.at[pl.ds(dest_ref[base + r * TOP_K + k], 1)],
                                  buf.at[pl.ds(k * tm + r, 1)], sem).start()
        return 0

    lax.fori_loop(0, tm, issue, 0, unroll=4)
    pltpu.make_async_copy(y_ref.at[pl.ds(0, TOP_K * tm)], buf, sem).wait()
    gate = gate_ref[...]
    acc = gate[:, 0:1] * buf[0:tm, :]
    for k in range(1, TOP_K):
        acc = acc + gate[:, k:k + 1] * buf[k * tm:(k + 1) * tm, :]
    o_ref[...] = x_ref[...] + mod_ref[0][5:6] * acc


def _combine(dest, y, x2, gate, mod, seq):
    t, d = x2.shape
    tm = min(256, seq)
    per_b = seq // tm
    return pl.pallas_call(
        functools.partial(_combine_kernel, tm=tm),
        grid_spec=pltpu.PrefetchScalarGridSpec(
            num_scalar_prefetch=1,
            grid=(t // tm,),
            in_specs=[pl.BlockSpec(memory_space=pl.ANY),
                      pl.BlockSpec((tm, d), lambda i, dst: (i, 0)),
                      pl.BlockSpec((tm, LANE), lambda i, dst: (i, 0)),
                      pl.BlockSpec((1, 6, d), lambda i, dst: (i // per_b, 0, 0))],
            out_specs=pl.BlockSpec((tm, d), lambda i, dst: (i, 0)),
            scratch_shapes=[pltpu.VMEM((TOP_K * tm, d), F32), pltpu.SemaphoreType.DMA]),
        out_shape=jax.ShapeDtypeStruct((t, d), F32),
        compiler_params=_cp(("arbitrary",)),
        name="moe_combine",
    )(dest, y, x2, gate, mod)


def _moe(x2, g, mod, rw, rb, w1, b1, w2, b2, seq, l):
    t, d = x2.shape
    _, n_exp, _, two_f = w1.shape
    d_ff = two_f // 2
    rw_p = jnp.pad(rw, ((0, 0), (0, LANE - n_exp))).astype(BF16)
    rb_p = jnp.pad(rb, (0, LANE - n_exp)).reshape(1, LANE)
    hp, idx, gate, rank, cnt = _router(x2, g, mod, rw_p, rb_p, seq, n_exp)

    per_blk = MOE_BLK // MOE_SUB
    counts = cnt[0, :n_exp]
    live_sub = (counts + MOE_SUB - 1) // MOE_SUB
    n_blk = (counts + MOE_BLK - 1) // MOE_BLK
    b_end = jnp.cumsum(n_blk)
    b_start = b_end - n_blk
    top_idx = idx[:, :TOP_K]
    dest = (b_start[top_idx] * MOE_BLK + rank[:, :TOP_K]).reshape(-1).astype(I32)
    n_assign = t * TOP_K
    n_blocks = -(-(n_assign + n_exp * (MOE_BLK - 1)) // MOE_BLK)
    n_rows = n_blocks * MOE_BLK
    tok = (jnp.arange(n_assign, dtype=I32) // TOP_K)
    row_tok = jnp.zeros((n_rows,), I32).at[dest].set(tok)
    blocks = jnp.arange(n_blocks, dtype=I32)
    block_exp = jnp.minimum(jnp.sum(b_end[None, :] <= blocks[:, None], axis=1), n_exp - 1).astype(I32)
    block_live = jnp.clip(live_sub[block_exp] - per_blk * (blocks - b_start[block_exp]), 0, per_blk).astype(I32)
    subs = jnp.arange(n_blocks * per_blk, dtype=I32)
    sub_live = (subs % per_blk < block_live[subs // per_blk]).astype(I32)
    has_rows = counts > 0
    experts = jnp.arange(n_exp, dtype=I32)
    later_exp = lax.cummin(jnp.where(has_rows, experts, n_exp)[::-1])[::-1]
    next_run_exp = jnp.concatenate([later_exp[1:], jnp.full((1,), n_exp, I32)])
    next_run_exp = jnp.where(next_run_exp >= n_exp, later_exp[0], next_run_exp)
    run_of_exp = jnp.cumsum(has_rows.astype(I32)) - 1
    live = block_live > 0
    run_first = (live & (blocks == b_start[block_exp])).astype(I32)
    misc = jnp.stack([b_end[-1], jnp.sum(has_rows.astype(I32))]).astype(I32)
    sched = (block_exp, block_live, run_first, run_of_exp[block_exp].astype(I32),
             next_run_exp[block_exp].astype(I32), misc)

    xs = _gather_rows(hp, row_tok, sub_live)
    act = _expert_up(sched, xs, w1, b1.reshape(n_exp, 1, two_f), d_ff, l)
    y = _expert_down(sched, act, w2, b2.reshape(n_exp, 1, d), l)
    return _combine(dest, y, x2, gate, mod, seq)


def _token_mixer(x2, l, mod, bsz, seq, norm1_g, w_in, gate_b, a_qn_g, a_kn_g, a_lam_q1, a_lam_k1,
                 a_lam_q2, a_lam_k2, a_subln_g, c_qn_g, c_kn_g, d_qn_g, d_kn_g, w_branch, w_out):
    d = x2.shape[1]
    slopes = [s * LOG2E for s in _alibi_slopes()]
    h = _norm1(x2, norm1_g, mod, seq)
    dq_end = DK_BLOCK * LANE - IDX_HEADS * IDX_DH
    iq_lo = dq_end + 2 * D_DH
    iq_hi = iq_lo + IDX_HEADS * IDX_DH
    w_qkv = jnp.concatenate([w_in[:, :dq_end], w_in[:, iq_lo:iq_hi], w_in[:, dq_end:iq_lo],
                             w_in[:, iq_hi:QKV_COLS], jnp.zeros((d, QKV_PAD - QKV_COLS), F32)],
                            axis=1).astype(BF16)
    w_gl = w_in[:, QKV_COLS:].astype(BF16)
    gl = _matmul(h, w_gl, BF16, 1024, 512, "in_proj_gate")

    ones = jnp.ones((LANE,), F32)
    gains = jnp.stack([
        jnp.tile(a_qn_g, 2) * (A_DH ** -0.5 * LOG2E),
        jnp.tile(a_kn_g, 2),
        ones,
        ones * (B_DH ** -0.5),
        ones, ones,
        c_qn_g * (C_DH ** -0.5 * LOG2E),
        c_kn_g,
        ones,
        d_qn_g * (D_DH ** -0.5 * LOG2E),
        ones,
        d_kn_g,
        ones, ones, ones, ones]).astype(F32)
    qkv, wraw = _qkv_proj(h, w_qkv, gains)

    lambda_init = 0.8 - 0.6 * math.exp(-0.3 * l)
    lam = (jnp.exp(jnp.sum(a_lam_q1 * a_lam_k1)) - jnp.exp(jnp.sum(a_lam_q2 * a_lam_k2)) + lambda_init)
    scal = jnp.stack([lam, jnp.float32(1.0 - lambda_init)]).astype(F32)

    oa = _diff_attention(qkv, scal, a_subln_g, bsz, seq, slopes[0::3])
    ob = _stick_attention(qkv, bsz, seq)
    oc = _moba_attention(qkv, bsz, seq, slopes[1::3])
    od = _dsa_attention(qkv, wraw, bsz, seq, slopes[2::3])

    y = _merge([oa, ob, oc, od], w_branch.astype(BF16), gl, gate_b)
    return _out_proj(y, w_out.astype(BF16), x2, mod, seq)


def kernel(x, c, ada_w, ada_b, norm1_g, norm2_g, w_in, gate_b, a_qn_g, a_kn_g, a_lam_q1, a_lam_k1,
           a_lam_q2, a_lam_k2, a_subln_g, c_qn_g, c_kn_g, d_qn_g, d_kn_g, w_branch, w_out,
           router_w, router_b, w1, b1, w2, b2):
    bsz, seq, d = x.shape
    depth = ada_w.shape[0]
    mods = _ada_mod(c, ada_w, ada_b)
    x2 = x.reshape(bsz * seq, d)
    for l in range(depth):
        mod = mods[l]
        x2 = _token_mixer(x2, l, mod, bsz, seq, norm1_g[l], w_in[l], gate_b[l], a_qn_g[l], a_kn_g[l],
                          a_lam_q1[l], a_lam_k1[l], a_lam_q2[l], a_lam_k2[l], a_subln_g[l],
                          c_qn_g[l], c_kn_g[l], d_qn_g[l], d_kn_g[l], w_branch[l], w_out[l])
        x2 = _moe(x2, norm2_g[l], mod, router_w[l], router_b[l], w1, b1[l], w2, b2[l], seq, l)
    return x2.reshape(bsz, seq, d)
```
